```python
import math
import jax, jax.numpy as jnp
from jax import lax
import numpy as np

D_MODEL = 2048
BATCH = 4
SEQ = 2048
DEPTH = 4
DEC_BATCH = 128
DEC_SEQ = 8
PAST_LEN = 16384
PAGE_SIZE = 128

MIX_WIDTH = D_MODEL
S5_WIDTH = MIX_WIDTH // 2
S5_GROUP = 16
S5_GROUPS = S5_WIDTH // S5_GROUP
S5_STATE = 64
HGRN_WIDTH = MIX_WIDTH - S5_WIDTH
HGRN_DK = 128
HGRN_DV = 128
HGRN_HEADS = HGRN_WIDTH // HGRN_DK
HGRN_CHUNK = 64
D_FF = 128 * ((8 * D_MODEL // 3 + 127) // 128)
IN_WIDTH = S5_WIDTH + 4 * HGRN_WIDTH
EPS = 1e-6
S5_DT_MIN = 1e-3
S5_DT_MAX = 1e-1

kernel_name = 'hymba_s5_hgrn2_macaron_decode_step'


def rms_norm(x, gain):
    x32 = x.astype(jnp.float32)
    y = x32 * lax.rsqrt(jnp.mean(x32 * x32, axis=-1, keepdims=True) + EPS)
    return (y * gain.astype(jnp.float32)).astype(x.dtype)


def swiglu(x, w_gate, w_up, w_down):
    return (jax.nn.silu(x @ w_gate) * (x @ w_up)) @ w_down


def s5_discretise(lam_re, lam_im, log_dt, b_re, b_im):
    lam_re = lam_re.astype(jnp.float32)
    lam_im = lam_im.astype(jnp.float32)
    dt = jnp.exp(log_dt.astype(jnp.float32))[:, None]
    mag = jnp.exp(lam_re * dt)
    ang = lam_im * dt
    abar_re = mag * jnp.cos(ang)
    abar_im = mag * jnp.sin(ang)
    p = abar_re - 1.0
    qv = abar_im
    den = lam_re * lam_re + lam_im * lam_im
    z_re = (p * lam_re + qv * lam_im) / den
    z_im = (qv * lam_re - p * lam_im) / den
    b_re = b_re.astype(jnp.float32)
    b_im = b_im.astype(jnp.float32)
    bb_re = z_re[..., None] * b_re - z_im[..., None] * b_im
    bb_im = z_re[..., None] * b_im + z_im[..., None] * b_re
    return abar_re, abar_im, bb_re, bb_im


def _complex_affine_combine(e1, e2):
    a1r, a1i, b1r, b1i = e1
    a2r, a2i, b2r, b2i = e2
    return (a2r * a1r - a2i * a1i,
            a2r * a1i + a2i * a1r,
            a2r * b1r - a2i * b1i + b2r,
            a2r * b1i + a2i * b1r + b2i)


def s5_scan(u, h_re, h_im, abar_re, abar_im, bb_re, bb_im, c_re, c_im):
    bu_re = jnp.einsum('btgc,gnc->btgn', u, bb_re)
    bu_im = jnp.einsum('btgc,gnc->btgn', u, bb_im)
    init_re = abar_re * h_re - abar_im * h_im
    init_im = abar_re * h_im + abar_im * h_re
    bu_re = bu_re.at[:, 0].add(init_re)
    bu_im = bu_im.at[:, 0].add(init_im)
    a_re = jnp.broadcast_to(abar_re, bu_re.shape)
    a_im = jnp.broadcast_to(abar_im, bu_im.shape)
    _, _, x_re, x_im = lax.associative_scan(_complex_affine_combine,
                                            (a_re, a_im, bu_re, bu_im), axis=1)
    y = (jnp.einsum('btgn,gcn->btgc', x_re, c_re.astype(jnp.float32))
         - jnp.einsum('btgn,gcn->btgc', x_im, c_im.astype(jnp.float32)))
    return y, x_re[:, -1], x_im[:, -1]


def hgrn_lower_bounds(lb_param):
    p = jax.nn.softmax(lb_param.astype(jnp.float32), axis=0)
    return jnp.cumsum(p, axis=0) - p[0]


def chunk_gla(q, k, v, log_f, s0):
    B, T, H, dk = q.shape
    dv = v.shape[-1]
    C = math.gcd(T, HGRN_CHUNK)
    n = T // C

    def to_chunks(t):
        return t.reshape(B, n, C, H, t.shape[-1]).transpose(1, 0, 3, 2, 4)

    causal = jnp.tril(jnp.ones((C, C), dtype=bool))[:, :, None]

    def step(S, inp):
        qi, ki, vi, fi = inp
        b = jnp.cumsum(fi, axis=2)
        o_inter = jnp.einsum('bhtk,bhkv->bhtv', qi * jnp.exp(b), S)
        diff = b[:, :, :, None, :] - b[:, :, None, :, :]
        decay = jnp.where(causal, jnp.exp(jnp.where(causal, diff, 0.0)), 0.0)
        att = jnp.einsum('bhtsk,bhsk->bhts', qi[:, :, :, None, :] * decay, ki)
        o = o_inter + jnp.einsum('bhts,bhsv->bhtv', att, vi)
        b_last = b[:, :, -1:, :]
        S = (jnp.exp(b_last[:, :, 0, :])[..., None] * S
             + jnp.einsum('bhsk,bhsv->bhkv', ki * jnp.exp(b_last - b), vi))
        return S, o

    S, o = lax.scan(step, s0, (to_chunks(q), to_chunks(k), to_chunks(v), to_chunks(log_f)))
    o = o.transpose(1, 0, 3, 2, 4).reshape(B, T, H, dv)
    return o, S


def hgrn2(q, f, v, g, lb, s0, gain):
    B, T, _ = q.shape
    shp = (B, T, HGRN_HEADS, HGRN_DK)
    qh = jax.nn.silu(q.astype(jnp.float32)).reshape(shp)
    f_gate = lb + (1.0 - lb) * jax.nn.sigmoid(f.astype(jnp.float32))
    log_f = jnp.log(f_gate).reshape(shp)
    k = (1.0 - f_gate).reshape(shp)
    vh = v.astype(jnp.float32).reshape(B, T, HGRN_HEADS, HGRN_DV)
    o, s_new = chunk_gla(qh, k, vh, log_f, s0)
    o = o * lax.rsqrt(jnp.mean(o * o, axis=-1, keepdims=True) + EPS)
    o = o * gain.astype(jnp.float32).reshape(HGRN_HEADS, HGRN_DV)
    o = o.reshape(B, T, HGRN_WIDTH) * jax.nn.silu(g.astype(jnp.float32))
    return o, s_new


def mixer(a, s_re, s_im, s_h, lb, p, l):
    B, T, _ = a.shape
    z = a @ p['w_in'][l]
    u, q, f, v, g = jnp.split(z, [S5_WIDTH, S5_WIDTH + HGRN_WIDTH,
                                  S5_WIDTH + 2 * HGRN_WIDTH,
                                  S5_WIDTH + 3 * HGRN_WIDTH], axis=-1)
    abar_re, abar_im, bb_re, bb_im = s5_discretise(p['s5_lam_re'][l], p['s5_lam_im'][l],
                                                   p['s5_log_dt'][l], p['s5_b_re'][l],
                                                   p['s5_b_im'][l])
    u32 = u.astype(jnp.float32)
    ys, s_re_new, s_im_new = s5_scan(u32.reshape(B, T, S5_GROUPS, S5_GROUP),
                                     s_re.astype(jnp.float32), s_im.astype(jnp.float32),
                                     abar_re, abar_im, bb_re, bb_im,
                                     p['s5_c_re'][l], p['s5_c_im'][l])
    ys = ys.reshape(B, T, S5_WIDTH) + p['s5_d'][l].astype(jnp.float32) * u32
    ys = jax.nn.gelu(ys).astype(a.dtype)
    ys = ys * jax.nn.sigmoid(ys @ p['s5_w_glu'][l] + p['s5_b_glu'][l])
    yh, s_h_new = hgrn2(q, f, v, g, lb, s_h.astype(jnp.float32), p['hgrn_norm'][l])
    out = jnp.concatenate([ys, yh.astype(a.dtype)], axis=-1) @ p['w_out'][l]
    return out, s_re_new, s_im_new, s_h_new


def trunk(x, s5_re0, s5_im0, hgrn0, p):
    lb_all = hgrn_lower_bounds(p['hgrn_lb'])
    h = x
    new_re, new_im, new_h = [], [], []
    for l in range(DEPTH):
        a = swiglu(rms_norm(h, p['norm_pre'][l, 0]), p['ffn1_w_gate'][l],
                   p['ffn1_w_up'][l], p['ffn1_w_down'][l])
        h = h + 0.5 * rms_norm(a, p['norm_post'][l, 0])
        a, sre, sim, sh = mixer(rms_norm(h, p['norm_pre'][l, 1]), s5_re0[l], s5_im0[l],
                                hgrn0[l], lb_all[l], p, l)
        h = h + rms_norm(a, p['norm_post'][l, 1])
        a = swiglu(rms_norm(h, p['norm_pre'][l, 2]), p['ffn2_w_gate'][l],
                   p['ffn2_w_up'][l], p['ffn2_w_down'][l])
        h = h + 0.5 * rms_norm(a, p['norm_post'][l, 2])
        new_re.append(sre)
        new_im.append(sim)
        new_h.append(sh)
    return h, jnp.stack(new_re), jnp.stack(new_im), jnp.stack(new_h)


def setup_inputs(seed: int = 0) -> dict:
    key = jax.random.key(seed)
    ks = jax.random.split(key, 32)
    f32 = jnp.float32
    nrm = lambda k, shape, s: jax.random.normal(k, shape, f32) * s
    G, N, Cg = S5_GROUPS, S5_STATE, S5_GROUP
    lam_im = (jnp.pi * jnp.arange(N, dtype=f32))[None, None, :] + nrm(ks[20], (DEPTH, G, N), 0.01)
    return {
        'x_prompt': nrm(ks[0], (BATCH, SEQ, D_MODEL), 1.0),
        'x_sample': nrm(ks[1], (DEC_BATCH, DEC_SEQ, D_MODEL), 1.0),
        'state_s5_re': nrm(ks[2], (DEPTH, DEC_BATCH, G, N), 0.3),
        'state_s5_im': nrm(ks[3], (DEPTH, DEC_BATCH, G, N), 0.3),
        'state_hgrn': nrm(ks[4], (DEPTH, DEC_BATCH, HGRN_HEADS, HGRN_DK, HGRN_DV), 0.5),
        'norm_pre': 1.0 + nrm(ks[5], (DEPTH, 3, D_MODEL), 0.02),
        'norm_post': 1.0 + nrm(ks[6], (DEPTH, 3, D_MODEL), 0.02),
        'ffn1_w_gate': nrm(ks[7], (DEPTH, D_MODEL, D_FF), D_MODEL ** -0.5),
        'ffn1_w_up': nrm(ks[8], (DEPTH, D_MODEL, D_FF), D_MODEL ** -0.5),
        'ffn1_w_down': nrm(ks[9], (DEPTH, D_FF, D_MODEL), D_FF ** -0.5),
        'ffn2_w_gate': nrm(ks[10], (DEPTH, D_MODEL, D_FF), D_MODEL ** -0.5),
        'ffn2_w_up': nrm(ks[11], (DEPTH, D_MODEL, D_FF), D_MODEL ** -0.5),
        'ffn2_w_down': nrm(ks[12], (DEPTH, D_FF, D_MODEL), D_FF ** -0.5),
        'w_in': nrm(ks[13], (DEPTH, D_MODEL, IN_WIDTH), D_MODEL ** -0.5),
        'w_out': nrm(ks[14], (DEPTH, MIX_WIDTH, D_MODEL), MIX_WIDTH ** -0.5),
        's5_lam_re': -0.5 + nrm(ks[15], (DEPTH, G, N), 0.01),
        's5_lam_im': lam_im,
        's5_log_dt': jax.random.uniform(ks[16], (DEPTH, G), f32,
                                        math.log(S5_DT_MIN), math.log(S5_DT_MAX)),
        's5_b_re': nrm(ks[17], (DEPTH, G, N, Cg), (2 * Cg) ** -0.5),
        's5_b_im': nrm(ks[18], (DEPTH, G, N, Cg), (2 * Cg) ** -0.5),
        's5_c_re': nrm(ks[19], (DEPTH, G, Cg, N), N ** -0.5),
        's5_c_im': nrm(ks[21], (DEPTH, G, Cg, N), N ** -0.5),
        's5_d': nrm(ks[22], (DEPTH, S5_WIDTH), 1.0),
        's5_w_glu': nrm(ks[23], (DEPTH, S5_WIDTH, S5_WIDTH), S5_WIDTH ** -0.5),
        's5_b_glu': nrm(ks[24], (DEPTH, S5_WIDTH), 0.01),
        'hgrn_lb': nrm(ks[25], (DEPTH, HGRN_WIDTH), 0.1),
        'hgrn_norm': 1.0 + nrm(ks[26], (DEPTH, HGRN_WIDTH), 0.02),
    }


def reference(x_prompt, x_sample, state_s5_re, state_s5_im, state_hgrn,
              norm_pre, norm_post,
              ffn1_w_gate, ffn1_w_up, ffn1_w_down,
              ffn2_w_gate, ffn2_w_up, ffn2_w_down,
              w_in, w_out,
              s5_lam_re, s5_lam_im, s5_log_dt, s5_b_re, s5_b_im, s5_c_re, s5_c_im,
              s5_d, s5_w_glu, s5_b_glu,
              hgrn_lb, hgrn_norm):
    p = dict(norm_pre=norm_pre, norm_post=norm_post,
             ffn1_w_gate=ffn1_w_gate, ffn1_w_up=ffn1_w_up, ffn1_w_down=ffn1_w_down,
             ffn2_w_gate=ffn2_w_gate, ffn2_w_up=ffn2_w_up, ffn2_w_down=ffn2_w_down,
             w_in=w_in, w_out=w_out,
             s5_lam_re=s5_lam_re, s5_lam_im=s5_lam_im, s5_log_dt=s5_log_dt,
             s5_b_re=s5_b_re, s5_b_im=s5_b_im, s5_c_re=s5_c_re, s5_c_im=s5_c_im,
             s5_d=s5_d, s5_w_glu=s5_w_glu, s5_b_glu=s5_b_glu,
             hgrn_lb=hgrn_lb, hgrn_norm=hgrn_norm)
    sdt = state_hgrn.dtype
    z_s5 = jnp.zeros((DEPTH, BATCH, S5_GROUPS, S5_STATE), jnp.float32)
    z_h = jnp.zeros((DEPTH, BATCH, HGRN_HEADS, HGRN_DK, HGRN_DV), jnp.float32)
    y_prompt, re_p, im_p, h_p = trunk(x_prompt, z_s5, z_s5, z_h, p)
    y_sample, re_s, im_s, h_s = trunk(x_sample, state_s5_re, state_s5_im, state_hgrn, p)
    return (y_prompt, y_sample,
            re_p.astype(sdt), im_p.astype(sdt), h_p.astype(sdt),
            re_s.astype(sdt), im_s.astype(sdt), h_s.astype(sdt))
```

```python
import functools

import jax
import jax.numpy as jnp
from jax import lax
from jax.experimental import pallas as pl
from jax.experimental.pallas import tpu as pltpu

F32 = jnp.float32
BF16 = jnp.bfloat16

D_MODEL = 2048
S5_WIDTH = 1024
S5_GROUP = 16
S5_GROUPS = 64
S5_STATE = 64
HGRN_WIDTH = 1024
HGRN_DK = 128
HGRN_DV = 128
HGRN_HEADS = 8
D_FF = 5504
IN_WIDTH = S5_WIDTH + 4 * HGRN_WIDTH
EPS = 1e-6

LANES = 128
SUBLANES = 8
VMEM_LIMIT_BYTES = 56 * 1024 * 1024

FF_TILE = 512
D_FF_PAD = FF_TILE * pl.cdiv(D_FF, FF_TILE)
IN_TILE = 1024
S5_GB = 8
S5_NBLK = S5_GROUPS // S5_GB
S5_SW = S5_GB * S5_STATE
S5_CW = S5_GB * S5_GROUP
GLA_ROWS = 64
GLA_DIAG = 7


def _params(*sem):
    return pltpu.CompilerParams(dimension_semantics=sem, vmem_limit_bytes=VMEM_LIMIT_BYTES)


def _row_tile(m):
    for t in (512, 256, 128, 64):
        if m % t == 0:
            return t
    raise ValueError(f"token count {m} must be a multiple of 64")


def _rms(x, gain):
    return x * lax.rsqrt(jnp.mean(x * x, axis=-1, keepdims=True) + EPS) * gain


def _dot(a, b):
    return jnp.dot(a, b, preferred_element_type=F32)


def _ffn_kernel(x_ref, gpre_ref, gpost_ref, wg_ref, wu_ref, wd_ref, o_ref, xn_ref, acc_ref, *, nj):
    j = pl.program_id(1)

    @pl.when(j == 0)
    def _():
        xn_ref[...] = _rms(x_ref[...], gpre_ref[...]).astype(BF16)
        acc_ref[...] = jnp.zeros_like(acc_ref)

    xn = xn_ref[...]
    g = _dot(xn, wg_ref[...])
    u = _dot(xn, wu_ref[...])
    a = (jax.nn.silu(g) * u).astype(BF16)
    acc_ref[...] += _dot(a, wd_ref[...])

    @pl.when(j == nj - 1)
    def _():
        o_ref[...] = x_ref[...] + 0.5 * _rms(acc_ref[...], gpost_ref[...])


def _ffn(h, gpre, gpost, wg, wu, wd):
    m = h.shape[0]
    tm = _row_tile(m)
    nj = D_FF_PAD // FF_TILE
    return pl.pallas_call(
        functools.partial(_ffn_kernel, nj=nj),
        grid=(m // tm, nj),
        in_specs=[
            pl.BlockSpec((tm, D_MODEL), lambda i, j: (i, 0)),
            pl.BlockSpec((1, D_MODEL), lambda i, j: (0, 0)),
            pl.BlockSpec((1, D_MODEL), lambda i, j: (0, 0)),
            pl.BlockSpec((D_MODEL, FF_TILE), lambda i, j: (0, j)),
            pl.BlockSpec((D_MODEL, FF_TILE), lambda i, j: (0, j)),
            pl.BlockSpec((FF_TILE, D_MODEL), lambda i, j: (j, 0)),
        ],
        out_specs=pl.BlockSpec((tm, D_MODEL), lambda i, j: (i, 0)),
        out_shape=jax.ShapeDtypeStruct((m, D_MODEL), F32),
        scratch_shapes=[pltpu.VMEM((tm, D_MODEL), BF16), pltpu.VMEM((tm, D_MODEL), F32)],
        compiler_params=_params("parallel", "arbitrary"),
        name="ffn",
    )(h, gpre, gpost, wg, wu, wd)


def _mixin_kernel(x_ref, gpre_ref, w_ref, o_ref, xn_ref):
    @pl.when(pl.program_id(1) == 0)
    def _():
        xn_ref[...] = _rms(x_ref[...], gpre_ref[...]).astype(BF16)

    o_ref[...] = _dot(xn_ref[...], w_ref[...])


def _mixin(h, gpre, w_in):
    m = h.shape[0]
    tm = _row_tile(m)
    return pl.pallas_call(
        _mixin_kernel,
        grid=(m // tm, IN_WIDTH // IN_TILE),
        in_specs=[
            pl.BlockSpec((tm, D_MODEL), lambda i, j: (i, 0)),
            pl.BlockSpec((1, D_MODEL), lambda i, j: (0, 0)),
            pl.BlockSpec((D_MODEL, IN_TILE), lambda i, j: (0, j)),
        ],
        out_specs=pl.BlockSpec((tm, IN_TILE), lambda i, j: (i, j)),
        out_shape=jax.ShapeDtypeStruct((m, IN_WIDTH), F32),
        scratch_shapes=[pltpu.VMEM((tm, D_MODEL), BF16)],
        compiler_params=_params("parallel", "arbitrary"),
        name="mixin",
    )(h, gpre, w_in)


def _s5_disc_kernel(lre_ref, lim_ref, ldt_ref, bre_ref, bim_ref,
                    are_ref, aim_ref, bbre_ref, bbim_ref):
    lam_re = lre_ref[...]
    lam_im = lim_ref[...]
    dt = jnp.exp(ldt_ref[...])
    mag = jnp.exp(lam_re * dt)
    ang = lam_im * dt
    abar_re = mag * jnp.cos(ang)
    abar_im = mag * jnp.sin(ang)
    p = abar_re - 1.0
    den = lam_re * lam_re + lam_im * lam_im
    z_re = (p * lam_re + abar_im * lam_im) / den
    z_im = (abar_im * lam_re - p * lam_im) / den
    are_ref[...] = abar_re
    aim_ref[...] = abar_im
    b_re = bre_ref[...]
    b_im = bim_ref[...]
    bbre_ref[...] = z_re * b_re - z_im * b_im
    bbim_ref[...] = z_re * b_im + z_im * b_re


def _s5_discretise(lam_re, lam_im, log_dt, b_re, b_im):
    depth = lam_re.shape[0]
    gn = S5_GROUPS * S5_STATE
    flat = lambda a: a.reshape(depth, 1, gn)
    ldt = jnp.broadcast_to(log_dt[:, :, None], lam_re.shape)
    chan_major = lambda b: jnp.transpose(b, (0, 3, 1, 2)).reshape(depth, S5_GROUP, gn)
    row = pl.BlockSpec((None, 1, gn), lambda l: (l, 0, 0))
    mat = pl.BlockSpec((None, S5_GROUP, gn), lambda l: (l, 0, 0))
    return pl.pallas_call(
        _s5_disc_kernel,
        grid=(depth,),
        in_specs=[row, row, row, mat, mat],
        out_specs=[row, row, mat, mat],
        out_shape=[jax.ShapeDtypeStruct((depth, 1, gn), F32)] * 2
        + [jax.ShapeDtypeStruct((depth, S5_GROUP, gn), F32)] * 2,
        compiler_params=_params("parallel"),
        name="s5_discretise",
    )(flat(lam_re), flat(lam_im), flat(ldt), chan_major(b_re), chan_major(b_im))


def _s5_block_weights(bb_re, bb_im, c_re, c_im):
    depth = bb_re.shape[0]
    eye = jnp.eye(S5_GB, dtype=F32)

    def in_map(bb):
        bb = bb.reshape(depth, S5_GROUP, S5_NBLK, S5_GB, S5_STATE)
        w = jnp.einsum("lcbgn,gh->lbgchn", bb, eye)
        return w.reshape(depth, S5_NBLK, S5_CW, S5_SW)

    def out_map(c):
        c = c.reshape(depth, S5_NBLK, S5_GB, S5_GROUP, S5_STATE)
        w = jnp.einsum("lbgcn,gh->lbgnhc", c, eye)
        return w.reshape(depth, S5_NBLK, S5_SW, S5_CW)

    bw = jnp.concatenate([in_map(bb_re), in_map(bb_im)], axis=-1).astype(BF16)
    cw = jnp.concatenate([out_map(c_re), -out_map(c_im)], axis=-2).astype(BF16)
    return bw, cw


def _s5_kernel(*refs, nseg, seglen, chained, ntc):
    if chained:
        (u_ref, bw_ref, cw_ref, d_ref, are_ref, aim_ref,
         y_ref, sre_ref, sim_ref, up_ref, x_ref, yp_ref, carry_ref) = refs
    else:
        (u_ref, bw_ref, cw_ref, d_ref, are_ref, aim_ref, h0re_ref, h0im_ref,
         y_ref, sre_ref, sim_ref, up_ref, x_ref, yp_ref) = refs
    sw = S5_SW

    def gather_rows(i, _):
        up_ref[pl.ds(pl.multiple_of(i * nseg, SUBLANES), nseg), :] = (
            u_ref[pl.ds(i, nseg, stride=seglen), :])
        return 0

    lax.fori_loop(0, seglen, gather_rows, 0)
    up = up_ref[...]
    x_ref[...] = _dot(up.astype(BF16), bw_ref[...])

    ar = are_ref[...]
    ai = aim_ref[...]
    arb = jnp.broadcast_to(ar, (SUBLANES, sw))
    aib = jnp.broadcast_to(ai, (SUBLANES, sw))

    def step(xr, xi, r0):
        br = x_ref[pl.ds(r0, SUBLANES), :sw]
        bi = x_ref[pl.ds(r0, SUBLANES), sw:]
        return arb * xr - aib * xi + br, arb * xi + aib * xr + bi

    if chained:
        tc = pl.program_id(2)

        @pl.when(tc == 0)
        def _():
            carry_ref[...] = jnp.zeros_like(carry_ref)

        def local_end(i, c):
            return step(c[0], c[1], pl.multiple_of(i * SUBLANES, SUBLANES))

        zero = jnp.zeros((SUBLANES, sw), F32)
        er, ei = lax.fori_loop(0, seglen, local_end, (zero, zero))
        pr, pi = ar, ai
        for _ in range(seglen.bit_length() - 1):
            pr, pi = pr * pr - pi * pi, 2.0 * pr * pi
        cr = carry_ref[0:1, :sw]
        ci = carry_ref[0:1, sw:]
        starts_r, starts_i = [], []
        for j in range(SUBLANES):
            starts_r.append(cr)
            starts_i.append(ci)
            cr, ci = (pr * cr - pi * ci + er[j:j + 1, :],
                      pr * ci + pi * cr + ei[j:j + 1, :])
        carry_ref[0:1, :sw] = cr
        carry_ref[0:1, sw:] = ci

        def scan(i, c):
            r0 = pl.multiple_of(i * SUBLANES, SUBLANES)
            xr, xi = step(c[0], c[1], r0)
            x_ref[pl.ds(r0, SUBLANES), :sw] = xr
            x_ref[pl.ds(r0, SUBLANES), sw:] = xi
            return xr, xi

        lax.fori_loop(0, seglen, scan,
                      (jnp.concatenate(starts_r, axis=0), jnp.concatenate(starts_i, axis=0)))

        @pl.when(tc == ntc - 1)
        def _():
            sre_ref[...] = carry_ref[0:1, :sw]
            sim_ref[...] = carry_ref[0:1, sw:]
    else:
        def scan_group(sg, _):
            g0 = pl.multiple_of(sg * SUBLANES, SUBLANES)
            xr = h0re_ref[pl.ds(g0, SUBLANES), :]
            xi = h0im_ref[pl.ds(g0, SUBLANES), :]
            for i in range(seglen):
                r0 = pl.multiple_of(i * nseg + g0, SUBLANES)
                xr, xi = step(xr, xi, r0)
                x_ref[pl.ds(r0, SUBLANES), :sw] = xr
                x_ref[pl.ds(r0, SUBLANES), sw:] = xi
            sre_ref[pl.ds(g0, SUBLANES), :] = xr
            sim_ref[pl.ds(g0, SUBLANES), :] = xi
            return 0

        lax.fori_loop(0, nseg // SUBLANES, scan_group, 0)

    y = _dot(x_ref[...].astype(BF16), cw_ref[...]) + d_ref[...] * up
    yp_ref[...] = jax.nn.gelu(y)

    def scatter_rows(i, _):
        y_ref[pl.ds(i, nseg, stride=seglen), :] = (
            yp_ref[pl.ds(pl.multiple_of(i * nseg, SUBLANES), nseg), :])
        return 0

    lax.fori_loop(0, seglen, scatter_rows, 0)


def _s5_prompt(z, bw, cw, d, are, aim, batch, seq):
    nseg, seglen = SUBLANES, 64
    rows = nseg * seglen
    ntc = seq // rows
    gn = S5_GROUPS * S5_STATE
    wspec = lambda shape: pl.BlockSpec((None,) + shape, lambda b, g, t: (g, 0, 0))
    vec = lambda w: pl.BlockSpec((1, w), lambda b, g, t: (0, g))
    st = pl.BlockSpec((None, 1, S5_SW), lambda b, g, t: (b, 0, g))
    y, sre, sim = pl.pallas_call(
        functools.partial(_s5_kernel, nseg=nseg, seglen=seglen, chained=True, ntc=ntc),
        grid=(batch, S5_NBLK, ntc),
        in_specs=[
            pl.BlockSpec((rows, S5_CW), lambda b, g, t: (b * ntc + t, g)),
            wspec((S5_CW, 2 * S5_SW)),
            wspec((2 * S5_SW, S5_CW)),
            vec(S5_CW), vec(S5_SW), vec(S5_SW),
        ],
        out_specs=[pl.BlockSpec((rows, S5_CW), lambda b, g, t: (b * ntc + t, g)), st, st],
        out_shape=[jax.ShapeDtypeStruct((batch * seq, S5_WIDTH), F32),
                   jax.ShapeDtypeStruct((batch, 1, gn), F32),
                   jax.ShapeDtypeStruct((batch, 1, gn), F32)],
        scratch_shapes=[pltpu.VMEM((rows, S5_CW), F32),
                        pltpu.VMEM((rows, 2 * S5_SW), F32),
                        pltpu.VMEM((rows, S5_CW), F32),
                        pltpu.VMEM((SUBLANES, 2 * S5_SW), F32)],
        compiler_params=_params("parallel", "parallel", "arbitrary"),
        name="s5_prompt",
    )(z, bw, cw, d, are, aim)
    shape = (batch, S5_GROUPS, S5_STATE)
    return y, sre.reshape(shape), sim.reshape(shape)


def _s5_sample(z, bw, cw, d, are, aim, h0_re, h0_im, row0, nb, nt):
    rows = nb * nt
    gn = S5_GROUPS * S5_STATE
    rb = row0 // rows
    wspec = lambda shape: pl.BlockSpec((None,) + shape, lambda g: (g, 0, 0))
    vec = lambda w: pl.BlockSpec((1, w), lambda g: (0, g))
    st = pl.BlockSpec((nb, S5_SW), lambda g: (0, g))
    y, sre, sim = pl.pallas_call(
        functools.partial(_s5_kernel, nseg=nb, seglen=nt, chained=False, ntc=1),
        grid=(S5_NBLK,),
        in_specs=[
            pl.BlockSpec((rows, S5_CW), lambda g: (rb, g)),
            wspec((S5_CW, 2 * S5_SW)),
            wspec((2 * S5_SW, S5_CW)),
            vec(S5_CW), vec(S5_SW), vec(S5_SW), st, st,
        ],
        out_specs=[pl.BlockSpec((rows, S5_CW), lambda g: (0, g)), st, st],
        out_shape=[jax.ShapeDtypeStruct((rows, S5_WIDTH), F32),
                   jax.ShapeDtypeStruct((nb, gn), F32),
                   jax.ShapeDtypeStruct((nb, gn), F32)],
        scratch_shapes=[pltpu.VMEM((rows, S5_CW), F32),
                        pltpu.VMEM((rows, 2 * S5_SW), F32),
                        pltpu.VMEM((rows, S5_CW), F32)],
        compiler_params=_params("parallel"),
        name="s5_sample",
    )(z, bw, cw, d, are, aim, h0_re.reshape(nb, gn), h0_im.reshape(nb, gn))
    shape = (nb, S5_GROUPS, S5_STATE)
    return y, sre.reshape(shape), sim.reshape(shape)


def _split3(x):
    hi = x.astype(BF16)
    r = x - hi.astype(F32)
    mid = r.astype(BF16)
    lo = (r - mid.astype(F32)).astype(BF16)
    return hi, mid, lo


def _level_ref(b, level):
    half = 1 << level
    span = 2 * half
    if span >= SUBLANES:
        b3 = b.reshape(GLA_ROWS // span, span, LANES)
        return jnp.broadcast_to(b3[:, half - 1:half, :], b3.shape).reshape(GLA_ROWS, LANES)
    b3 = b.reshape(GLA_ROWS // SUBLANES, SUBLANES, LANES)
    r = lax.broadcasted_iota(jnp.int32, b3.shape, 1)
    nspan = SUBLANES // span
    ref = jnp.broadcast_to(b3[:, (nspan - 1) * span + half - 1:(nspan - 1) * span + half, :], b3.shape)
    for p in range(nspan - 2, -1, -1):
        piece = jnp.broadcast_to(b3[:, p * span + half - 1:p * span + half, :], b3.shape)
        ref = jnp.where(r < (p + 1) * span, piece, ref)
    return ref.reshape(GLA_ROWS, LANES)


def _gla_tables(seq_len):
    t = lax.broadcasted_iota(jnp.int32, (GLA_ROWS, GLA_ROWS), 0)
    s = lax.broadcasted_iota(jnp.int32, (GLA_ROWS, GLA_ROWS), 1)
    x = t ^ s
    level = jnp.zeros((GLA_ROWS, GLA_ROWS), jnp.int32)
    for k in range(1, 6):
        level = level + jnp.where(x >= (1 << k), 1, 0)
    level = jnp.where(s < t, level, jnp.where(s == t, GLA_DIAG, -1))
    tri = jnp.where((s <= t) & (x < seq_len), 1.0, 0.0).astype(BF16)
    rows = lax.broadcasted_iota(jnp.int32, (GLA_ROWS, LANES), 0)
    return level, tri, rows


def _gla_block(q, fz, v, lb, states, tables, seq_len):
    level, tri, rows = tables
    nseq = GLA_ROWS // seq_len
    nlev = seq_len.bit_length() - 1
    qc = jax.nn.silu(q)
    fg = lb + (1.0 - lb) * jax.nn.sigmoid(fz)
    lf = jnp.log(fg)
    kc = 1.0 - fg
    vb = v.astype(BF16)
    hi, mid, lo = _split3(lf)
    b = _dot(tri, hi) + _dot(tri, mid) + _dot(tri, lo)

    nt = (((1,), (1,)), ((), ()))
    att = jnp.where(level == GLA_DIAG,
                    lax.dot_general(qc.astype(BF16), kc.astype(BF16), nt, preferred_element_type=F32),
                    0.0)
    for lev in range(nlev):
        w = jnp.exp(-jnp.abs(b - _level_ref(b, lev)))
        upper = ((rows >> lev) & 1) == 1
        qh = jnp.where(upper, qc * w, 0.0).astype(BF16)
        kh = jnp.where(upper, 0.0, kc * w).astype(BF16)
        att = jnp.where(level == lev,
                        lax.dot_general(qh, kh, nt, preferred_element_type=F32), att)
    o = _dot(att.astype(BF16), vb)

    b3 = b.reshape(nseq, seq_len, LANES)
    blast = jnp.broadcast_to(b3[:, seq_len - 1:seq_len, :], b3.shape).reshape(GLA_ROWS, LANES)
    qe = (qc * jnp.exp(b)).astype(BF16)
    kd = (kc * jnp.exp(blast - b)).astype(BF16)
    tn = (((0,), (0,)), ((), ()))
    o_inter, new_states = [], []
    for n in range(nseq):
        sl = slice(n * seq_len, (n + 1) * seq_len)
        s_n = states[n]
        o_inter.append(_dot(qe[sl], s_n.astype(BF16)))
        upd = lax.dot_general(kd[sl], vb[sl], tn, preferred_element_type=F32)
        e_row = jnp.exp(blast[n * seq_len:n * seq_len + 1, :])
        e_col = jnp.transpose(jnp.broadcast_to(e_row, (HGRN_DK, LANES)))
        new_states.append(e_col * s_n + upd)
    o = o + (o_inter[0] if nseq == 1 else jnp.concatenate(o_inter, axis=0))
    return o, new_states


def _lower_bound(lbp, layer):
    if layer == 0:
        return jnp.zeros((1, lbp.shape[1]), F32)
    e = jnp.exp(lbp - jnp.max(lbp, axis=0, keepdims=True))
    p = e / jnp.sum(e, axis=0, keepdims=True)
    return jnp.sum(p[1:layer + 1, :], axis=0, keepdims=True)


def _gla_finish(o, g, gain):
    o = o * lax.rsqrt(jnp.mean(o * o, axis=-1, keepdims=True) + EPS)
    return o * gain * jax.nn.silu(g)


def _hgrn_prompt_kernel(q_ref, f_ref, v_ref, g_ref, lbp_ref, gain_ref, o_ref, sout_ref, s_ref,
                        *, layer, nchunk, ntb):
    tb = pl.program_id(2)

    @pl.when(tb == 0)
    def _():
        s_ref[...] = jnp.zeros_like(s_ref)

    tables = _gla_tables(GLA_ROWS)
    lb = _lower_bound(lbp_ref[...], layer)
    gain = gain_ref[layer:layer + 1, :]

    def chunk(c, _):
        r = pl.ds(pl.multiple_of(c * GLA_ROWS, GLA_ROWS), GLA_ROWS)
        o, (s_new,) = _gla_block(q_ref[r, :], f_ref[r, :], v_ref[r, :], lb, [s_ref[...]],
                                 tables, GLA_ROWS)
        s_ref[...] = s_new
        o_ref[r, :] = _gla_finish(o, g_ref[r, :], gain)
        return 0

    lax.fori_loop(0, nchunk, chunk, 0)

    @pl.when(tb == ntb - 1)
    def _():
        sout_ref[...] = s_ref[...]


def _hgrn_prompt(z, lbp, gain, layer, batch, seq):
    tb_rows = 512
    ntb = seq // tb_rows
    hcol = S5_WIDTH // LANES

    def zspec(k):
        return pl.BlockSpec((tb_rows, LANES), lambda b, h, t: (b * ntb + t, hcol + k * HGRN_HEADS + h))

    par = pl.BlockSpec((lbp.shape[0], LANES), lambda b, h, t: (0, h))
    return pl.pallas_call(
        functools.partial(_hgrn_prompt_kernel, layer=layer, nchunk=tb_rows // GLA_ROWS, ntb=ntb),
        grid=(batch, HGRN_HEADS, ntb),
        in_specs=[zspec(0), zspec(1), zspec(2), zspec(3), par, par],
        out_specs=[pl.BlockSpec((tb_rows, LANES), lambda b, h, t: (b * ntb + t, h)),
                   pl.BlockSpec((None, None, HGRN_DK, HGRN_DV), lambda b, h, t: (b, h, 0, 0))],
        out_shape=[jax.ShapeDtypeStruct((batch * seq, HGRN_WIDTH), F32),
                   jax.ShapeDtypeStruct((batch, HGRN_HEADS, HGRN_DK, HGRN_DV), F32)],
        scratch_shapes=[pltpu.VMEM((HGRN_DK, HGRN_DV), F32)],
        compiler_params=_params("parallel", "parallel", "arbitrary"),
        name="hgrn_prompt",
    )(z, z, z, z, lbp, gain)


def _hgrn_sample_kernel(q_ref, f_ref, v_ref, g_ref, lbp_ref, gain_ref, s0_ref, o_ref, sout_ref,
                        *, layer, nt):
    nseq = GLA_ROWS // nt
    tables = _gla_tables(nt)
    lb = _lower_bound(lbp_ref[...], layer)
    gain = gain_ref[layer:layer + 1, :]
    o, new_states = _gla_block(q_ref[...], f_ref[...], v_ref[...], lb,
                               [s0_ref[n] for n in range(nseq)], tables, nt)
    for n in range(nseq):
        sout_ref[n] = new_states[n]
    o_ref[...] = _gla_finish(o, g_ref[...], gain)


def _hgrn_sample(z, lbp, gain, s0, layer, row0, nb, nt):
    nseq = GLA_ROWS // nt
    rb0 = row0 // GLA_ROWS
    hcol = S5_WIDTH // LANES

    def zspec(k):
        return pl.BlockSpec((GLA_ROWS, LANES), lambda h, i: (rb0 + i, hcol + k * HGRN_HEADS + h))

    par = pl.BlockSpec((lbp.shape[0], LANES), lambda h, i: (0, h))
    sspec = pl.BlockSpec((nseq, None, HGRN_DK, HGRN_DV), lambda h, i: (i, h, 0, 0))
    return pl.pallas_call(
        functools.partial(_hgrn_sample_kernel, layer=layer, nt=nt),
        grid=(HGRN_HEADS, nb // nseq),
        in_specs=[zspec(0), zspec(1), zspec(2), zspec(3), par, par, sspec],
        out_specs=[pl.BlockSpec((GLA_ROWS, LANES), lambda h, i: (i, h)), sspec],
        out_shape=[jax.ShapeDtypeStruct((nb * nt, HGRN_WIDTH), F32),
                   jax.ShapeDtypeStruct(s0.shape, F32)],
        compiler_params=_params("parallel", "parallel"),
        name="hgrn_sample",
    )(z, z, z, z, lbp, gain, s0)


def _mixout_kernel(h_ref, ys_ref, yh_ref, wglu_ref, bglu_ref, wo_ref, gpost_ref, o_ref):
    ys = ys_ref[...]
    gate = jax.nn.sigmoid(_dot(ys.astype(BF16), wglu_ref[...]) + bglu_ref[...])
    out = (_dot((ys * gate).astype(BF16), wo_ref[:S5_WIDTH, :])
           + _dot(yh_ref[...].astype(BF16), wo_ref[S5_WIDTH:, :]))
    o_ref[...] = h_ref[...] + _rms(out, gpost_ref[...])


def _mixout(h, ys, yh, w_glu, b_glu, w_out, gpost):
    m = h.shape[0]
    tm = _row_tile(m)
    full = lambda shape: pl.BlockSpec(shape, lambda i: (0, 0))
    return pl.pallas_call(
        _mixout_kernel,
        grid=(m // tm,),
        in_specs=[
            pl.BlockSpec((tm, D_MODEL), lambda i: (i, 0)),
            pl.BlockSpec((tm, S5_WIDTH), lambda i: (i, 0)),
            pl.BlockSpec((tm, HGRN_WIDTH), lambda i: (i, 0)),
            full((S5_WIDTH, S5_WIDTH)), full((1, S5_WIDTH)),
            full((D_MODEL, D_MODEL)), full((1, D_MODEL)),
        ],
        out_specs=pl.BlockSpec((tm, D_MODEL), lambda i: (i, 0)),
        out_shape=jax.ShapeDtypeStruct((m, D_MODEL), F32),
        compiler_params=_params("parallel"),
        name="mixout",
    )(h, ys, yh, w_glu, b_glu, w_out, gpost)


def kernel(x_prompt, x_sample, state_s5_re, state_s5_im, state_hgrn, norm_pre, norm_post, ffn1_w_gate, ffn1_w_up, ffn1_w_down, ffn2_w_gate, ffn2_w_up, ffn2_w_down, w_in, w_out, s5_lam_re, s5_lam_im, s5_log_dt, s5_b_re, s5_b_im, s5_c_re, s5_c_im, s5_d, s5_w_glu, s5_b_glu, hgrn_lb, hgrn_norm):
    batch, seq, _ = x_prompt.shape
    nb, nt, _ = x_sample.shape
    depth = norm_pre.shape[0]
    mp = batch * seq
    sdt = state_hgrn.dtype

    h = jnp.concatenate([x_prompt.reshape(mp, D_MODEL), x_sample.reshape(nb * nt, D_MODEL)], axis=0)

    are, aim, bb_re, bb_im = _s5_discretise(s5_lam_re, s5_lam_im, s5_log_dt, s5_b_re, s5_b_im)
    bw, cw = _s5_block_weights(bb_re, bb_im, s5_c_re, s5_c_im)

    pad_cols = lambda w: jnp.pad(w.astype(BF16), ((0, 0), (0, D_FF_PAD - D_FF)))
    pad_rows = lambda w: jnp.pad(w.astype(BF16), ((0, D_FF_PAD - D_FF), (0, 0)))
    vec = lambda a: a.reshape(1, -1)

    outs = {k: [] for k in ("re_p", "im_p", "h_p", "re_s", "im_s", "h_s")}
    for l in range(depth):
        h = _ffn(h, vec(norm_pre[l, 0]), vec(norm_post[l, 0]),
                 pad_cols(ffn1_w_gate[l]), pad_cols(ffn1_w_up[l]), pad_rows(ffn1_w_down[l]))

        z = _mixin(h, vec(norm_pre[l, 1]), w_in[l].astype(BF16))
        d = vec(s5_d[l])
        ys_p, re_p, im_p = _s5_prompt(z, bw[l], cw[l], d, are[l], aim[l], batch, seq)
        ys_s, re_s, im_s = _s5_sample(z, bw[l], cw[l], d, are[l], aim[l],
                                      state_s5_re[l], state_s5_im[l], mp, nb, nt)
        yh_p, h_p = _hgrn_prompt(z, hgrn_lb, hgrn_norm, l, batch, seq)
        yh_s, h_s = _hgrn_sample(z, hgrn_lb, hgrn_norm, state_hgrn[l], l, mp, nb, nt)
        h = _mixout(h, jnp.concatenate([ys_p, ys_s], axis=0), jnp.concatenate([yh_p, yh_s], axis=0),
                    s5_w_glu[l].astype(BF16), vec(s5_b_glu[l]), w_out[l].astype(BF16),
                    vec(norm_post[l, 1]))

        h = _ffn(h, vec(norm_pre[l, 2]), vec(norm_post[l, 2]),
                 pad_cols(ffn2_w_gate[l]), pad_cols(ffn2_w_up[l]), pad_rows(ffn2_w_down[l]))
        for k, a in zip(("re_p", "im_p", "h_p", "re_s", "im_s", "h_s"),
                        (re_p, im_p, h_p, re_s, im_s, h_s)):
            outs[k].append(a)

    stack = lambda k: jnp.stack(outs[k]).astype(sdt)
    return (h[:mp].reshape(batch, seq, D_MODEL), h[mp:].reshape(nb, nt, D_MODEL),
            stack("re_p"), stack("im_p"), stack("h_p"),
            stack("re_s"), stack("im_s"), stack("h_s"))
```

```python
import functools

import jax
import jax.numpy as jnp
from jax import lax
from jax.experimental import pallas as pl
from jax.experimental.pallas import tpu as pltpu

F32 = jnp.float32
BF16 = jnp.bfloat16

D_MODEL = 2048
S5_WIDTH = 1024
S5_GROUP = 16
S5_GROUPS = 64
S5_STATE = 64
HGRN_WIDTH = 1024
HGRN_DK = 128
HGRN_DV = 128
HGRN_HEADS = 8
D_FF = 5504
IN_WIDTH = S5_WIDTH + 4 * HGRN_WIDTH
EPS = 1e-6

LANES = 128
SUBLANES = 8
VMEM_LIMIT_BYTES = 56 * 1024 * 1024

TOKEN_TILE = 1024
FF_TILE = 256
IN_TILE = 512
HGRN_HB = 4
S5_GB = 8
S5_NBLK = S5_GROUPS // S5_GB
S5_SW = S5_GB * S5_STATE
S5_CW = S5_GB * S5_GROUP
GLA_ROWS = 64
GLA_DIAG = 7


def _params(*sem):
    return pltpu.CompilerParams(dimension_semantics=sem, vmem_limit_bytes=VMEM_LIMIT_BYTES)


def _row_tile(m, largest=512):
    for t in (1024, 512, 256, 128, 64):
        if t <= largest and m % t == 0:
            return t
    raise ValueError(f"token count {m} must be a multiple of 64")


def _rms(x, gain):
    return x * lax.rsqrt(jnp.mean(x * x, axis=-1, keepdims=True) + EPS) * gain


def _dot(a, b):
    return jnp.dot(a, b, preferred_element_type=F32)


def _ffn_kernel(x_ref, gpre_ref, gpost_ref, wg_ref, wu_ref, wd_ref, o_ref, xn_ref, *, nj):
    j = pl.program_id(1)

    @pl.when(j == 0)
    def _():
        xn_ref[...] = _rms(x_ref[...], gpre_ref[...]).astype(BF16)
        o_ref[...] = jnp.zeros_like(o_ref)

    nvalid = D_FF - j * FF_TILE
    xn = xn_ref[...]
    g = _dot(xn, wg_ref[...].astype(BF16))
    u = _dot(xn, wu_ref[...].astype(BF16))
    col = lax.broadcasted_iota(jnp.int32, g.shape, 1)
    a = jnp.where(col < nvalid, jax.nn.silu(g) * u, 0.0).astype(BF16)
    row = lax.broadcasted_iota(jnp.int32, wd_ref.shape, 0)
    wd = jnp.where(row < nvalid, wd_ref[...], 0.0).astype(BF16)
    o_ref[...] += _dot(a, wd)

    @pl.when(j == nj - 1)
    def _():
        o_ref[...] = x_ref[...] + 0.5 * _rms(o_ref[...], gpost_ref[...])


def _ffn(h, gpre, gpost, wg, wu, wd, layer):
    m = h.shape[0]
    tm = _row_tile(m, TOKEN_TILE)
    nj = pl.cdiv(D_FF, FF_TILE)
    return pl.pallas_call(
        functools.partial(_ffn_kernel, nj=nj),
        grid=(m // tm, nj),
        in_specs=[
            pl.BlockSpec((tm, D_MODEL), lambda i, j: (i, 0), pipeline_mode=pl.Buffered(1)),
            pl.BlockSpec((1, D_MODEL), lambda i, j: (0, 0)),
            pl.BlockSpec((1, D_MODEL), lambda i, j: (0, 0)),
            pl.BlockSpec((None, D_MODEL, FF_TILE), lambda i, j: (layer, 0, j)),
            pl.BlockSpec((None, D_MODEL, FF_TILE), lambda i, j: (layer, 0, j)),
            pl.BlockSpec((None, FF_TILE, D_MODEL), lambda i, j: (layer, j, 0)),
        ],
        out_specs=pl.BlockSpec((tm, D_MODEL), lambda i, j: (i, 0)),
        out_shape=jax.ShapeDtypeStruct((m, D_MODEL), F32),
        scratch_shapes=[pltpu.VMEM((tm, D_MODEL), BF16)],
        compiler_params=_params("parallel", "arbitrary"),
        name="ffn",
    )(h, gpre, gpost, wg, wu, wd)


def _mixin_kernel(x_ref, gpre_ref, w_ref, o_ref, xn_ref):
    @pl.when(pl.program_id(1) == 0)
    def _():
        xn_ref[...] = _rms(x_ref[...], gpre_ref[...]).astype(BF16)

    o_ref[...] = _dot(xn_ref[...], w_ref[...].astype(BF16))


def _mixin(h, gpre, w_in, layer):
    m = h.shape[0]
    tm = _row_tile(m, TOKEN_TILE)
    return pl.pallas_call(
        _mixin_kernel,
        grid=(m // tm, IN_WIDTH // IN_TILE),
        in_specs=[
            pl.BlockSpec((tm, D_MODEL), lambda i, j: (i, 0), pipeline_mode=pl.Buffered(1)),
            pl.BlockSpec((1, D_MODEL), lambda i, j: (0, 0)),
            pl.BlockSpec((None, D_MODEL, IN_TILE), lambda i, j: (layer, 0, j)),
        ],
        out_specs=pl.BlockSpec((tm, IN_TILE), lambda i, j: (i, j)),
        out_shape=jax.ShapeDtypeStruct((m, IN_WIDTH), F32),
        scratch_shapes=[pltpu.VMEM((tm, D_MODEL), BF16)],
        compiler_params=_params("parallel", "arbitrary"),
        name="mixin",
    )(h, gpre, w_in)


def _s5_disc_kernel(lre_ref, lim_ref, ldt_ref, bre_ref, bim_ref,
                    are_ref, aim_ref, bbre_ref, bbim_ref):
    lam_re = lre_ref[...]
    lam_im = lim_ref[...]
    dt = jnp.exp(ldt_ref[...])
    mag = jnp.exp(lam_re * dt)
    ang = lam_im * dt
    abar_re = mag * jnp.cos(ang)
    abar_im = mag * jnp.sin(ang)
    p = abar_re - 1.0
    den = lam_re * lam_re + lam_im * lam_im
    z_re = (p * lam_re + abar_im * lam_im) / den
    z_im = (abar_im * lam_re - p * lam_im) / den
    are_ref[...] = abar_re
    aim_ref[...] = abar_im
    b_re = bre_ref[...]
    b_im = bim_ref[...]
    bbre_ref[...] = z_re * b_re - z_im * b_im
    bbim_ref[...] = z_re * b_im + z_im * b_re


def _s5_discretise(lam_re, lam_im, log_dt, b_re, b_im):
    depth = lam_re.shape[0]
    gn = S5_GROUPS * S5_STATE
    flat = lambda a: a.reshape(depth, 1, gn)
    ldt = jnp.broadcast_to(log_dt[:, :, None], lam_re.shape)
    chan_major = lambda b: jnp.transpose(b, (0, 3, 1, 2)).reshape(depth, S5_GROUP, gn)
    row = pl.BlockSpec((None, 1, gn), lambda l: (l, 0, 0))
    mat = pl.BlockSpec((None, S5_GROUP, gn), lambda l: (l, 0, 0))
    return pl.pallas_call(
        _s5_disc_kernel,
        grid=(depth,),
        in_specs=[row, row, row, mat, mat],
        out_specs=[row, row, mat, mat],
        out_shape=[jax.ShapeDtypeStruct((depth, 1, gn), F32)] * 2
        + [jax.ShapeDtypeStruct((depth, S5_GROUP, gn), F32)] * 2,
        compiler_params=_params("parallel"),
        name="s5_discretise",
    )(flat(lam_re), flat(lam_im), flat(ldt), chan_major(b_re), chan_major(b_im))


def _s5_block_weights(bb_re, bb_im, c_re, c_im):
    depth = bb_re.shape[0]
    eye = jnp.eye(S5_GB, dtype=F32)

    def in_map(bb):
        bb = bb.reshape(depth, S5_GROUP, S5_NBLK, S5_GB, S5_STATE)
        w = jnp.einsum("lcbgn,gh->lbgchn", bb, eye)
        return w.reshape(depth, S5_NBLK, S5_CW, S5_SW)

    def out_map(c):
        c = c.reshape(depth, S5_NBLK, S5_GB, S5_GROUP, S5_STATE)
        w = jnp.einsum("lbgcn,gh->lbgnhc", c, eye)
        return w.reshape(depth, S5_NBLK, S5_SW, S5_CW)

    bw = jnp.concatenate([in_map(bb_re), in_map(bb_im)], axis=-1).astype(BF16)
    cw = jnp.concatenate([out_map(c_re), -out_map(c_im)], axis=-2).astype(BF16)
    return bw, cw


def _s5_kernel(*refs, nseg, seglen, chained, ntc):
    if chained:
        (u_ref, bw_ref, cw_ref, d_ref, are_ref, aim_ref,
         y_ref, sre_ref, sim_ref, up_ref, x_ref, yp_ref, carry_ref) = refs
    else:
        (u_ref, bw_ref, cw_ref, d_ref, are_ref, aim_ref, h0re_ref, h0im_ref, y_all_ref,
         y_ref, sre_ref, sim_ref, up_ref, x_ref, yp_ref) = refs
    sw = S5_SW
    unroll = min(seglen, 8)

    def gather_rows(i, _):
        up_ref[pl.ds(pl.multiple_of(i * nseg, SUBLANES), nseg), :] = (
            u_ref[pl.ds(i, nseg, stride=seglen), :])
        return 0

    lax.fori_loop(0, seglen, gather_rows, 0, unroll=unroll)
    up = up_ref[...]
    x_ref[...] = _dot(up.astype(BF16), bw_ref[...])

    ar = are_ref[...]
    ai = aim_ref[...]
    arb = jnp.broadcast_to(ar, (SUBLANES, sw))
    aib = jnp.broadcast_to(ai, (SUBLANES, sw))

    def step(xr, xi, r0):
        br = x_ref[pl.ds(r0, SUBLANES), :sw]
        bi = x_ref[pl.ds(r0, SUBLANES), sw:]
        return arb * xr - aib * xi + br, arb * xi + aib * xr + bi

    if chained:
        tc = pl.program_id(2)

        @pl.when(tc == 0)
        def _():
            carry_ref[...] = jnp.zeros_like(carry_ref)

        def local_end(i, c):
            return step(c[0], c[1], pl.multiple_of(i * SUBLANES, SUBLANES))

        zero = jnp.zeros((SUBLANES, sw), F32)
        er, ei = lax.fori_loop(0, seglen, local_end, (zero, zero), unroll=4)
        pr, pi = ar, ai
        for _ in range(seglen.bit_length() - 1):
            pr, pi = pr * pr - pi * pi, 2.0 * pr * pi
        cr = carry_ref[0:1, :sw]
        ci = carry_ref[0:1, sw:]
        starts_r, starts_i = [], []
        for j in range(SUBLANES):
            starts_r.append(cr)
            starts_i.append(ci)
            cr, ci = (pr * cr - pi * ci + er[j:j + 1, :],
                      pr * ci + pi * cr + ei[j:j + 1, :])
        carry_ref[0:1, :sw] = cr
        carry_ref[0:1, sw:] = ci

        def scan(i, c):
            r0 = pl.multiple_of(i * SUBLANES, SUBLANES)
            xr, xi = step(c[0], c[1], r0)
            x_ref[pl.ds(r0, SUBLANES), :sw] = xr
            x_ref[pl.ds(r0, SUBLANES), sw:] = xi
            return xr, xi

        lax.fori_loop(0, seglen, scan,
                      (jnp.concatenate(starts_r, axis=0), jnp.concatenate(starts_i, axis=0)),
                      unroll=4)

        @pl.when(tc == ntc - 1)
        def _():
            sre_ref[...] = carry_ref[0:1, :sw]
            sim_ref[...] = carry_ref[0:1, sw:]
    else:
        def scan_group(sg, _):
            g0 = pl.multiple_of(sg * SUBLANES, SUBLANES)
            xr = h0re_ref[pl.ds(g0, SUBLANES), :]
            xi = h0im_ref[pl.ds(g0, SUBLANES), :]
            for i in range(seglen):
                r0 = pl.multiple_of(i * nseg + g0, SUBLANES)
                xr, xi = step(xr, xi, r0)
                x_ref[pl.ds(r0, SUBLANES), :sw] = xr
                x_ref[pl.ds(r0, SUBLANES), sw:] = xi
            sre_ref[pl.ds(g0, SUBLANES), :] = xr
            sim_ref[pl.ds(g0, SUBLANES), :] = xi
            return 0

        lax.fori_loop(0, nseg // SUBLANES, scan_group, 0)

    y = _dot(x_ref[...].astype(BF16), cw_ref[...]) + d_ref[...] * up
    yp_ref[...] = jax.nn.gelu(y)

    def scatter_rows(i, _):
        y_ref[pl.ds(i, nseg, stride=seglen), :] = (
            yp_ref[pl.ds(pl.multiple_of(i * nseg, SUBLANES), nseg), :])
        return 0

    lax.fori_loop(0, seglen, scatter_rows, 0, unroll=unroll)


def _s5_prompt(z, bw, cw, d, are, aim, batch, seq):
    nseg, seglen = SUBLANES, 64
    rows = nseg * seglen
    ntc = seq // rows
    gn = S5_GROUPS * S5_STATE
    wspec = lambda shape: pl.BlockSpec((None,) + shape, lambda b, g, t: (g, 0, 0))
    vec = lambda w: pl.BlockSpec((1, w), lambda b, g, t: (0, g))
    st = pl.BlockSpec((None, 1, S5_SW), lambda b, g, t: (b, 0, g))
    y, sre, sim = pl.pallas_call(
        functools.partial(_s5_kernel, nseg=nseg, seglen=seglen, chained=True, ntc=ntc),
        grid=(batch, S5_NBLK, ntc),
        in_specs=[
            pl.BlockSpec((rows, S5_CW), lambda b, g, t: (b * ntc + t, g)),
            wspec((S5_CW, 2 * S5_SW)),
            wspec((2 * S5_SW, S5_CW)),
            vec(S5_CW), vec(S5_SW), vec(S5_SW),
        ],
        out_specs=[pl.BlockSpec((rows, S5_CW), lambda b, g, t: (b * ntc + t, g)), st, st],
        out_shape=[jax.ShapeDtypeStruct((z.shape[0], S5_WIDTH), F32),
                   jax.ShapeDtypeStruct((batch, 1, gn), F32),
                   jax.ShapeDtypeStruct((batch, 1, gn), F32)],
        scratch_shapes=[pltpu.VMEM((rows, S5_CW), F32),
                        pltpu.VMEM((rows, 2 * S5_SW), F32),
                        pltpu.VMEM((rows, S5_CW), F32),
                        pltpu.VMEM((SUBLANES, 2 * S5_SW), F32)],
        compiler_params=_params("parallel", "parallel", "arbitrary"),
        name="s5_prompt",
    )(z, bw, cw, d, are, aim)
    shape = (batch, S5_GROUPS, S5_STATE)
    return y, sre.reshape(shape), sim.reshape(shape)


def _s5_sample(z, y_all, bw, cw, d, are, aim, h0_re, h0_im, row0, nb, nt):
    rows = nb * nt
    gn = S5_GROUPS * S5_STATE
    rb = row0 // rows
    wspec = lambda shape: pl.BlockSpec((None,) + shape, lambda g: (g, 0, 0))
    vec = lambda w: pl.BlockSpec((1, w), lambda g: (0, g))
    st = pl.BlockSpec((nb, S5_SW), lambda g: (0, g))
    y, sre, sim = pl.pallas_call(
        functools.partial(_s5_kernel, nseg=nb, seglen=nt, chained=False, ntc=1),
        grid=(S5_NBLK,),
        in_specs=[
            pl.BlockSpec((rows, S5_CW), lambda g: (rb, g)),
            wspec((S5_CW, 2 * S5_SW)),
            wspec((2 * S5_SW, S5_CW)),
            vec(S5_CW), vec(S5_SW), vec(S5_SW), st, st,
            pl.BlockSpec(memory_space=pl.ANY),
        ],
        out_specs=[pl.BlockSpec((rows, S5_CW), lambda g: (rb, g)), st, st],
        out_shape=[jax.ShapeDtypeStruct(y_all.shape, F32),
                   jax.ShapeDtypeStruct((nb, gn), F32),
                   jax.ShapeDtypeStruct((nb, gn), F32)],
        input_output_aliases={8: 0},
        scratch_shapes=[pltpu.VMEM((rows, S5_CW), F32),
                        pltpu.VMEM((rows, 2 * S5_SW), F32),
                        pltpu.VMEM((rows, S5_CW), F32)],
        compiler_params=_params("parallel"),
        name="s5_sample",
    )(z, bw, cw, d, are, aim, h0_re.reshape(nb, gn), h0_im.reshape(nb, gn), y_all)
    shape = (nb, S5_GROUPS, S5_STATE)
    return y, sre.reshape(shape), sim.reshape(shape)


def _split3(x):
    hi = x.astype(BF16)
    r = x - hi.astype(F32)
    mid = r.astype(BF16)
    lo = (r - mid.astype(F32)).astype(BF16)
    return hi, mid, lo


def _level_ref(b, level):
    half = 1 << level
    span = 2 * half
    if span >= SUBLANES:
        b3 = b.reshape(GLA_ROWS // span, span, LANES)
        return jnp.broadcast_to(b3[:, half - 1:half, :], b3.shape).reshape(GLA_ROWS, LANES)
    b3 = b.reshape(GLA_ROWS // SUBLANES, SUBLANES, LANES)
    r = lax.broadcasted_iota(jnp.int32, b3.shape, 1)
    nspan = SUBLANES // span
    ref = jnp.broadcast_to(b3[:, (nspan - 1) * span + half - 1:(nspan - 1) * span + half, :], b3.shape)
    for p in range(nspan - 2, -1, -1):
        piece = jnp.broadcast_to(b3[:, p * span + half - 1:p * span + half, :], b3.shape)
        ref = jnp.where(r < (p + 1) * span, piece, ref)
    return ref.reshape(GLA_ROWS, LANES)


def _gla_tables(seq_len):
    t = lax.broadcasted_iota(jnp.int32, (GLA_ROWS, GLA_ROWS), 0)
    s = lax.broadcasted_iota(jnp.int32, (GLA_ROWS, GLA_ROWS), 1)
    x = t ^ s
    level = jnp.zeros((GLA_ROWS, GLA_ROWS), jnp.int32)
    for k in range(1, 6):
        level = level + jnp.where(x >= (1 << k), 1, 0)
    level = jnp.where(s < t, level, jnp.where(s == t, GLA_DIAG, -1))
    tri = jnp.where((s <= t) & (x < seq_len), 1.0, 0.0).astype(BF16)
    rows = lax.broadcasted_iota(jnp.int32, (GLA_ROWS, LANES), 0)
    return level, tri, rows


def _gla_block(q, fz, v, lb, states, tables, seq_len):
    level, tri, rows = tables
    nseq = GLA_ROWS // seq_len
    nlev = seq_len.bit_length() - 1
    qc = jax.nn.silu(q)
    fg = lb + (1.0 - lb) * jax.nn.sigmoid(fz)
    lf = jnp.log(fg)
    kc = 1.0 - fg
    vb = v.astype(BF16)
    hi, mid, lo = _split3(lf)
    b = _dot(tri, hi) + _dot(tri, mid) + _dot(tri, lo)

    nt = (((1,), (1,)), ((), ()))
    att = jnp.where(level == GLA_DIAG,
                    lax.dot_general(qc.astype(BF16), kc.astype(BF16), nt, preferred_element_type=F32),
                    0.0)
    for lev in range(nlev):
        w = jnp.exp(-jnp.abs(b - _level_ref(b, lev)))
        upper = ((rows >> lev) & 1) == 1
        qh = jnp.where(upper, qc * w, 0.0).astype(BF16)
        kh = jnp.where(upper, 0.0, kc * w).astype(BF16)
        att = jnp.where(level == lev,
                        lax.dot_general(qh, kh, nt, preferred_element_type=F32), att)
    o = _dot(att.astype(BF16), vb)

    b3 = b.reshape(nseq, seq_len, LANES)
    blast = jnp.broadcast_to(b3[:, seq_len - 1:seq_len, :], b3.shape).reshape(GLA_ROWS, LANES)
    qe = (qc * jnp.exp(b)).astype(BF16)
    kd = (kc * jnp.exp(blast - b)).astype(BF16)
    tn = (((0,), (0,)), ((), ()))
    o_inter, new_states = [], []
    for n in range(nseq):
        sl = slice(n * seq_len, (n + 1) * seq_len)
        s_n = states[n]
        o_inter.append(_dot(qe[sl], s_n.astype(BF16)))
        upd = lax.dot_general(kd[sl], vb[sl], tn, preferred_element_type=F32)
        e_row = jnp.exp(blast[n * seq_len:n * seq_len + 1, :])
        e_col = jnp.transpose(jnp.broadcast_to(e_row, (HGRN_DK, LANES)))
        new_states.append(e_col * s_n + upd)
    o = o + (o_inter[0] if nseq == 1 else jnp.concatenate(o_inter, axis=0))
    return o, new_states


def _lower_bound(lbp, layer):
    if layer == 0:
        return jnp.zeros((1, lbp.shape[1]), F32)
    e = jnp.exp(lbp - jnp.max(lbp, axis=0, keepdims=True))
    p = e / jnp.sum(e, axis=0, keepdims=True)
    return jnp.sum(p[1:layer + 1, :], axis=0, keepdims=True)


def _gla_finish(o, g, gain):
    o = o * lax.rsqrt(jnp.mean(o * o, axis=-1, keepdims=True) + EPS)
    return o * gain * jax.nn.silu(g)


def _hgrn_prompt_kernel(q_ref, f_ref, v_ref, g_ref, lbp_ref, gain_ref, o_ref, sout_ref, s_ref,
                        *, layer, nchunk, ntb):
    tb = pl.program_id(2)

    @pl.when(tb == 0)
    def _():
        s_ref[...] = jnp.zeros_like(s_ref)

    tables = _gla_tables(GLA_ROWS)
    lb = _lower_bound(lbp_ref[...], layer)
    gain = gain_ref[layer:layer + 1, :]

    def chunk(c, _):
        r = pl.ds(pl.multiple_of(c * GLA_ROWS, GLA_ROWS), GLA_ROWS)
        for hh in range(HGRN_HB):
            cs = slice(hh * LANES, (hh + 1) * LANES)
            o, (s_new,) = _gla_block(q_ref[r, cs], f_ref[r, cs], v_ref[r, cs], lb[:, cs],
                                     [s_ref[hh]], tables, GLA_ROWS)
            s_ref[hh] = s_new
            o_ref[r, cs] = _gla_finish(o, g_ref[r, cs], gain[:, cs])
        return 0

    lax.fori_loop(0, nchunk, chunk, 0)

    @pl.when(tb == ntb - 1)
    def _():
        sout_ref[...] = s_ref[...]


def _hgrn_prompt(z, lbp, gain, layer, batch, seq):
    tb_rows = 512
    ntb = seq // tb_rows
    wb = HGRN_HB * LANES
    nhb = HGRN_HEADS // HGRN_HB

    def zspec(k):
        c0 = (S5_WIDTH + k * HGRN_WIDTH) // wb
        return pl.BlockSpec((tb_rows, wb), lambda b, h, t: (b * ntb + t, c0 + h))

    par = pl.BlockSpec((lbp.shape[0], wb), lambda b, h, t: (0, h))
    return pl.pallas_call(
        functools.partial(_hgrn_prompt_kernel, layer=layer, nchunk=tb_rows // GLA_ROWS, ntb=ntb),
        grid=(batch, nhb, ntb),
        in_specs=[zspec(0), zspec(1), zspec(2), zspec(3), par, par],
        out_specs=[pl.BlockSpec((tb_rows, wb), lambda b, h, t: (b * ntb + t, h)),
                   pl.BlockSpec((None, HGRN_HB, HGRN_DK, HGRN_DV), lambda b, h, t: (b, h, 0, 0))],
        out_shape=[jax.ShapeDtypeStruct((z.shape[0], HGRN_WIDTH), F32),
                   jax.ShapeDtypeStruct((batch, HGRN_HEADS, HGRN_DK, HGRN_DV), F32)],
        scratch_shapes=[pltpu.VMEM((HGRN_HB, HGRN_DK, HGRN_DV), F32)],
        compiler_params=_params("parallel", "parallel", "arbitrary"),
        name="hgrn_prompt",
    )(z, z, z, z, lbp, gain)


def _hgrn_sample_kernel(q_ref, f_ref, v_ref, g_ref, lbp_ref, gain_ref, s0_ref, *rest, layer, nt):
    o_ref, sout_ref = rest[-2:]
    nseq = GLA_ROWS // nt
    tables = _gla_tables(nt)
    lb = _lower_bound(lbp_ref[...], layer)
    gain = gain_ref[layer:layer + 1, :]
    for hh in range(HGRN_HB):
        cs = slice(hh * LANES, (hh + 1) * LANES)
        o, new_states = _gla_block(q_ref[:, cs], f_ref[:, cs], v_ref[:, cs], lb[:, cs],
                                   [s0_ref[n, hh] for n in range(nseq)], tables, nt)
        for n in range(nseq):
            sout_ref[n, hh] = new_states[n].astype(sout_ref.dtype)
        o_ref[:, cs] = _gla_finish(o, g_ref[:, cs], gain[:, cs])


def _hgrn_sample(z, y_all, lbp, gain, s0, s_all, layer, row0, nb, nt):
    nseq = GLA_ROWS // nt
    rb0 = row0 // GLA_ROWS
    wb = HGRN_HB * LANES

    def zspec(k):
        c0 = (S5_WIDTH + k * HGRN_WIDTH) // wb
        return pl.BlockSpec((GLA_ROWS, wb), lambda h, i: (rb0 + i, c0 + h))

    par = pl.BlockSpec((lbp.shape[0], wb), lambda h, i: (0, h))
    sspec = pl.BlockSpec((None, nseq, HGRN_HB, HGRN_DK, HGRN_DV), lambda h, i: (layer, i, h, 0, 0))
    anyspec = pl.BlockSpec(memory_space=pl.ANY)
    carried = (y_all,) if s_all is None else (y_all, s_all)
    aliases = {7: 0} if s_all is None else {7: 0, 8: 1}
    return pl.pallas_call(
        functools.partial(_hgrn_sample_kernel, layer=layer, nt=nt),
        grid=(HGRN_HEADS // HGRN_HB, nb // nseq),
        in_specs=[zspec(0), zspec(1), zspec(2), zspec(3), par, par, sspec] + [anyspec] * len(carried),
        out_specs=[pl.BlockSpec((GLA_ROWS, wb), lambda h, i: (rb0 + i, h)), sspec],
        out_shape=[jax.ShapeDtypeStruct(y_all.shape, F32),
                   jax.ShapeDtypeStruct(s0.shape, s0.dtype)],
        input_output_aliases=aliases,
        compiler_params=_params("parallel", "parallel"),
        name="hgrn_sample",
    )(z, z, z, z, lbp, gain, s0, *carried)


def _mixout_kernel(h_ref, ys_ref, yh_ref, wglu_ref, bglu_ref, wo_ref, gpost_ref, o_ref):
    ys = ys_ref[...]
    gate = jax.nn.sigmoid(_dot(ys.astype(BF16), wglu_ref[...]) + bglu_ref[...])
    out = (_dot((ys * gate).astype(BF16), wo_ref[:S5_WIDTH, :])
           + _dot(yh_ref[...].astype(BF16), wo_ref[S5_WIDTH:, :]))
    o_ref[...] = h_ref[...] + _rms(out, gpost_ref[...])


def _mixout(h, ys, yh, w_glu, b_glu, w_out, gpost):
    m = h.shape[0]
    tm = _row_tile(m)
    full = lambda shape: pl.BlockSpec(shape, lambda i: (0, 0))
    return pl.pallas_call(
        _mixout_kernel,
        grid=(m // tm,),
        in_specs=[
            pl.BlockSpec((tm, D_MODEL), lambda i: (i, 0)),
            pl.BlockSpec((tm, S5_WIDTH), lambda i: (i, 0)),
            pl.BlockSpec((tm, HGRN_WIDTH), lambda i: (i, 0)),
            full((S5_WIDTH, S5_WIDTH)), full((1, S5_WIDTH)),
            full((D_MODEL, D_MODEL)), full((1, D_MODEL)),
        ],
        out_specs=pl.BlockSpec((tm, D_MODEL), lambda i: (i, 0)),
        out_shape=jax.ShapeDtypeStruct((m, D_MODEL), F32),
        compiler_params=_params("parallel"),
        name="mixout",
    )(h, ys, yh, w_glu, b_glu, w_out, gpost)


def kernel(x_prompt, x_sample, state_s5_re, state_s5_im, state_hgrn, norm_pre, norm_post, ffn1_w_gate, ffn1_w_up, ffn1_w_down, ffn2_w_gate, ffn2_w_up, ffn2_w_down, w_in, w_out, s5_lam_re, s5_lam_im, s5_log_dt, s5_b_re, s5_b_im, s5_c_re, s5_c_im, s5_d, s5_w_glu, s5_b_glu, hgrn_lb, hgrn_norm):
    batch, seq, _ = x_prompt.shape
    nb, nt, _ = x_sample.shape
    depth = norm_pre.shape[0]
    mp = batch * seq
    sdt = state_hgrn.dtype

    h = jnp.concatenate([x_prompt.reshape(mp, D_MODEL), x_sample.reshape(nb * nt, D_MODEL)], axis=0)

    are, aim, bb_re, bb_im = _s5_discretise(s5_lam_re, s5_lam_im, s5_log_dt, s5_b_re, s5_b_im)
    bw, cw = _s5_block_weights(bb_re, bb_im, s5_c_re, s5_c_im)

    vec = lambda a: a.reshape(1, -1)

    small = ("re_p", "im_p", "h_p", "re_s", "im_s")
    outs = {k: [] for k in small}
    h_s_all = None
    for l in range(depth):
        h = _ffn(h, vec(norm_pre[l, 0]), vec(norm_post[l, 0]),
                 ffn1_w_gate, ffn1_w_up, ffn1_w_down, l)

        z = _mixin(h, vec(norm_pre[l, 1]), w_in, l)
        d = vec(s5_d[l])
        ys, re_p, im_p = _s5_prompt(z, bw[l], cw[l], d, are[l], aim[l], batch, seq)
        ys, re_s, im_s = _s5_sample(z, ys, bw[l], cw[l], d, are[l], aim[l],
                                    state_s5_re[l], state_s5_im[l], mp, nb, nt)
        yh, h_p = _hgrn_prompt(z, hgrn_lb, hgrn_norm, l, batch, seq)
        yh, h_s_all = _hgrn_sample(z, yh, hgrn_lb, hgrn_norm, state_hgrn, h_s_all, l, mp, nb, nt)
        h = _mixout(h, ys, yh, s5_w_glu[l].astype(BF16), vec(s5_b_glu[l]), w_out[l].astype(BF16),
                    vec(norm_post[l, 1]))

        h = _ffn(h, vec(norm_pre[l, 2]), vec(norm_post[l, 2]),
                 ffn2_w_gate, ffn2_w_up, ffn2_w_down, l)
        for k, a in zip(small, (re_p, im_p, h_p, re_s, im_s)):
            outs[k].append(a)

    stack = lambda k: jnp.stack(outs[k]).astype(sdt)
    return (h[:mp].reshape(batch, seq, D_MODEL), h[mp:].reshape(nb, nt, D_MODEL),
            stack("re_p"), stack("im_p"), stack("h_p"),
            stack("re_s"), stack("im_s"), h_s_all)
```

```python
import functools

import jax
import jax.numpy as jnp
from jax import lax
from jax.experimental import pallas as pl
from jax.experimental.pallas import tpu as pltpu

F32 = jnp.float32
BF16 = jnp.bfloat16

D_MODEL = 2048
S5_WIDTH = 1024
S5_GROUP = 16
S5_GROUPS = 64
S5_STATE = 64
HGRN_WIDTH = 1024
HGRN_DK = 128
HGRN_DV = 128
HGRN_HEADS = 8
D_FF = 5504
IN_WIDTH = S5_WIDTH + 4 * HGRN_WIDTH
EPS = 1e-6

LANES = 128
SUBLANES = 8
VMEM_LIMIT_BYTES = 60 * 1024 * 1024

TOKEN_TILE = 1024
FF_TILE = 256
IN_TILE = 1024
NORM_ROWS = 256
HGRN_HB = 8
S5_GB = 8
S5_NBLK = S5_GROUPS // S5_GB
S5_STEP_BLOCKS = 2
S5_SW = S5_GB * S5_STATE
S5_CW = S5_GB * S5_GROUP
GLA_ROWS = 64
GLA_DIAG = 7


def _params(*sem):
    return pltpu.CompilerParams(dimension_semantics=sem, vmem_limit_bytes=VMEM_LIMIT_BYTES)


def _row_tile(m, largest=512):
    for t in (1024, 512, 256, 128, 64):
        if t <= largest and m % t == 0:
            return t
    raise ValueError(f"token count {m} must be a multiple of 64")


def _rms(x, gain):
    return x * lax.rsqrt(jnp.mean(x * x, axis=-1, keepdims=True) + EPS) * gain


def _dot(a, b):
    return jnp.dot(a, b, preferred_element_type=F32)


def _for_row_chunks(nrows, fn):
    step = min(NORM_ROWS, nrows)
    for c in range(nrows // step):
        fn(pl.ds(c * step, step))


def _ffn_kernel(x_ref, gpre_ref, gpost_ref, wg_ref, wu_ref, wd_ref, o_ref, xn_ref, *, nj):
    j = pl.program_id(1)

    @pl.when(j == 0)
    def _():
        def norm_in(r):
            xn_ref[r, :] = _rms(x_ref[r, :], gpre_ref[...]).astype(BF16)
            o_ref[r, :] = jnp.zeros((r.size, D_MODEL), F32)

        _for_row_chunks(x_ref.shape[0], norm_in)

    nvalid = D_FF - j * FF_TILE
    xn = xn_ref[...]
    g = _dot(xn, wg_ref[...].astype(BF16))
    u = _dot(xn, wu_ref[...].astype(BF16))
    col = lax.broadcasted_iota(jnp.int32, g.shape, 1)
    a = jnp.where(col < nvalid, jax.nn.silu(g) * u, 0.0).astype(BF16)
    row = lax.broadcasted_iota(jnp.int32, wd_ref.shape, 0)
    wd = jnp.where(row < nvalid, wd_ref[...], 0.0).astype(BF16)
    o_ref[...] += _dot(a, wd)

    @pl.when(j == nj - 1)
    def _():
        def norm_out(r):
            o_ref[r, :] = x_ref[r, :] + 0.5 * _rms(o_ref[r, :], gpost_ref[...])

        _for_row_chunks(x_ref.shape[0], norm_out)


def _ffn(h, gpre, gpost, wg, wu, wd, layer):
    m = h.shape[0]
    tm = _row_tile(m, TOKEN_TILE)
    nj = pl.cdiv(D_FF, FF_TILE)
    return pl.pallas_call(
        functools.partial(_ffn_kernel, nj=nj),
        grid=(m // tm, nj),
        in_specs=[
            pl.BlockSpec((tm, D_MODEL), lambda i, j: (i, 0)),
            pl.BlockSpec((1, D_MODEL), lambda i, j: (0, 0)),
            pl.BlockSpec((1, D_MODEL), lambda i, j: (0, 0)),
            pl.BlockSpec((None, D_MODEL, FF_TILE), lambda i, j: (layer, 0, j)),
            pl.BlockSpec((None, D_MODEL, FF_TILE), lambda i, j: (layer, 0, j)),
            pl.BlockSpec((None, FF_TILE, D_MODEL), lambda i, j: (layer, j, 0)),
        ],
        out_specs=pl.BlockSpec((tm, D_MODEL), lambda i, j: (i, 0)),
        out_shape=jax.ShapeDtypeStruct((m, D_MODEL), F32),
        scratch_shapes=[pltpu.VMEM((tm, D_MODEL), BF16)],
        compiler_params=_params("parallel", "arbitrary"),
        name="ffn",
    )(h, gpre, gpost, wg, wu, wd)


def _mixin_kernel(x_ref, gpre_ref, w_ref, o_ref, xn_ref):
    @pl.when(pl.program_id(1) == 0)
    def _():
        def norm_in(r):
            xn_ref[r, :] = _rms(x_ref[r, :], gpre_ref[...]).astype(BF16)

        _for_row_chunks(x_ref.shape[0], norm_in)

    o_ref[...] = _dot(xn_ref[...], w_ref[...].astype(BF16))


def _mixin(h, gpre, w_in, layer):
    m = h.shape[0]
    tm = _row_tile(m, TOKEN_TILE)
    return pl.pallas_call(
        _mixin_kernel,
        grid=(m // tm, IN_WIDTH // IN_TILE),
        in_specs=[
            pl.BlockSpec((tm, D_MODEL), lambda i, j: (i, 0)),
            pl.BlockSpec((1, D_MODEL), lambda i, j: (0, 0)),
            pl.BlockSpec((None, D_MODEL, IN_TILE), lambda i, j: (layer, 0, j)),
        ],
        out_specs=pl.BlockSpec((tm, IN_TILE), lambda i, j: (i, j)),
        out_shape=jax.ShapeDtypeStruct((m, IN_WIDTH), F32),
        scratch_shapes=[pltpu.VMEM((tm, D_MODEL), BF16)],
        compiler_params=_params("parallel", "arbitrary"),
        name="mixin",
    )(h, gpre, w_in)


def _s5_disc_kernel(lre_ref, lim_ref, ldt_ref, bre_ref, bim_ref,
                    are_ref, aim_ref, bbre_ref, bbim_ref):
    lam_re = lre_ref[...]
    lam_im = lim_ref[...]
    dt = jnp.exp(ldt_ref[...])
    mag = jnp.exp(lam_re * dt)
    ang = lam_im * dt
    abar_re = mag * jnp.cos(ang)
    abar_im = mag * jnp.sin(ang)
    p = abar_re - 1.0
    den = lam_re * lam_re + lam_im * lam_im
    z_re = (p * lam_re + abar_im * lam_im) / den
    z_im = (abar_im * lam_re - p * lam_im) / den
    are_ref[...] = abar_re
    aim_ref[...] = abar_im
    b_re = bre_ref[...]
    b_im = bim_ref[...]
    bbre_ref[...] = z_re * b_re - z_im * b_im
    bbim_ref[...] = z_re * b_im + z_im * b_re


def _s5_discretise(lam_re, lam_im, log_dt, b_re, b_im):
    depth = lam_re.shape[0]
    gn = S5_GROUPS * S5_STATE
    flat = lambda a: a.reshape(depth, 1, gn)
    ldt = jnp.broadcast_to(log_dt[:, :, None], lam_re.shape)
    chan_major = lambda b: jnp.transpose(b, (0, 3, 1, 2)).reshape(depth, S5_GROUP, gn)
    row = pl.BlockSpec((None, 1, gn), lambda l: (l, 0, 0))
    mat = pl.BlockSpec((None, S5_GROUP, gn), lambda l: (l, 0, 0))
    return pl.pallas_call(
        _s5_disc_kernel,
        grid=(depth,),
        in_specs=[row, row, row, mat, mat],
        out_specs=[row, row, mat, mat],
        out_shape=[jax.ShapeDtypeStruct((depth, 1, gn), F32)] * 2
        + [jax.ShapeDtypeStruct((depth, S5_GROUP, gn), F32)] * 2,
        compiler_params=_params("parallel"),
        name="s5_discretise",
    )(flat(lam_re), flat(lam_im), flat(ldt), chan_major(b_re), chan_major(b_im))


def _s5_block_weights(bb_re, bb_im, c_re, c_im):
    depth = bb_re.shape[0]
    eye = jnp.eye(S5_GB, dtype=F32)

    def in_map(bb):
        bb = bb.reshape(depth, S5_GROUP, S5_NBLK, S5_GB, S5_STATE)
        w = jnp.einsum("lcbgn,gh->lbgchn", bb, eye)
        return w.reshape(depth, S5_NBLK, S5_CW, S5_SW)

    def out_map(c):
        c = c.reshape(depth, S5_NBLK, S5_GB, S5_GROUP, S5_STATE)
        w = jnp.einsum("lbgcn,gh->lbgnhc", c, eye)
        return w.reshape(depth, S5_NBLK, S5_SW, S5_CW)

    bw = jnp.concatenate([in_map(bb_re), in_map(bb_im)], axis=-1).astype(BF16)
    cw = jnp.concatenate([out_map(c_re), -out_map(c_im)], axis=-2).astype(BF16)
    return bw, cw


def _s5_kernel(*refs, nseg, seglen, chained, ntc):
    if chained:
        (u_ref, bw_ref, cw_ref, d_ref, are_ref, aim_ref,
         y_ref, sre_ref, sim_ref, us_ref, up_ref, x_ref, yp_ref, carry_ref) = refs
    else:
        (u_ref, bw_ref, cw_ref, d_ref, are_ref, aim_ref, h0re_ref, h0im_ref, y_all_ref,
         y_ref, sre_ref, sim_ref, us_ref, up_ref, x_ref, yp_ref) = refs
    sw, cwid = S5_SW, S5_CW
    blocks = range(S5_STEP_BLOCKS)

    for k in blocks:
        us_ref[k] = u_ref[:, k * cwid:(k + 1) * cwid]
    for k in blocks:
        for i in range(seglen):
            up_ref[k, i * nseg:(i + 1) * nseg, :] = us_ref.at[k][pl.ds(i, nseg, stride=seglen), :]
        x_ref[k] = _dot(up_ref[k].astype(BF16), bw_ref[k])

    abar = []
    for k in blocks:
        ar = are_ref[:, k * sw:(k + 1) * sw]
        ai = aim_ref[:, k * sw:(k + 1) * sw]
        abar.append((ar, ai, jnp.broadcast_to(ar, (SUBLANES, sw)), jnp.broadcast_to(ai, (SUBLANES, sw))))

    def step(k, xr, xi, r0):
        _, _, arb, aib = abar[k]
        br = x_ref[k, pl.ds(r0, SUBLANES), :sw]
        bi = x_ref[k, pl.ds(r0, SUBLANES), sw:]
        return arb * xr - aib * xi + br, arb * xi + aib * xr + bi

    def step_store(k, xr, xi, r0):
        xr, xi = step(k, xr, xi, r0)
        x_ref[k, pl.ds(r0, SUBLANES), :sw] = xr
        x_ref[k, pl.ds(r0, SUBLANES), sw:] = xi
        return xr, xi

    if chained:
        tc = pl.program_id(2)

        @pl.when(tc == 0)
        def _():
            carry_ref[...] = jnp.zeros_like(carry_ref)

        for k in blocks:
            ar, ai, _, _ = abar[k]
            er = ei = jnp.zeros((SUBLANES, sw), F32)
            for i in range(seglen):
                er, ei = step(k, er, ei, i * SUBLANES)
            pr, pi = ar, ai
            for _ in range(seglen.bit_length() - 1):
                pr, pi = pr * pr - pi * pi, 2.0 * pr * pi
            cr = carry_ref[k, 0:1, :sw]
            ci = carry_ref[k, 0:1, sw:]
            starts_r, starts_i = [], []
            for j in range(SUBLANES):
                starts_r.append(cr)
                starts_i.append(ci)
                cr, ci = (pr * cr - pi * ci + er[j:j + 1, :],
                          pr * ci + pi * cr + ei[j:j + 1, :])
            carry_ref[k, 0:1, :sw] = cr
            carry_ref[k, 0:1, sw:] = ci
            xr = jnp.concatenate(starts_r, axis=0)
            xi = jnp.concatenate(starts_i, axis=0)
            for i in range(seglen):
                xr, xi = step_store(k, xr, xi, i * SUBLANES)

        @pl.when(tc == ntc - 1)
        def _():
            for k in blocks:
                sre_ref[:, k * sw:(k + 1) * sw] = carry_ref[k, 0:1, :sw]
                sim_ref[:, k * sw:(k + 1) * sw] = carry_ref[k, 0:1, sw:]
    else:
        def scan_group(sg, _):
            g0 = pl.multiple_of(sg * SUBLANES, SUBLANES)
            for k in blocks:
                xr = h0re_ref[pl.ds(g0, SUBLANES), k * sw:(k + 1) * sw]
                xi = h0im_ref[pl.ds(g0, SUBLANES), k * sw:(k + 1) * sw]
                for i in range(seglen):
                    xr, xi = step_store(k, xr, xi, pl.multiple_of(i * nseg + g0, SUBLANES))
                sre_ref[pl.ds(g0, SUBLANES), k * sw:(k + 1) * sw] = xr
                sim_ref[pl.ds(g0, SUBLANES), k * sw:(k + 1) * sw] = xi
            return 0

        lax.fori_loop(0, nseg // SUBLANES, scan_group, 0)

    for k in blocks:
        y = (_dot(x_ref[k].astype(BF16), cw_ref[k])
             + d_ref[:, k * cwid:(k + 1) * cwid] * up_ref[k])
        yp_ref[k] = jax.nn.gelu(y)
        for i in range(seglen):
            us_ref.at[k][pl.ds(i, nseg, stride=seglen), :] = yp_ref[k, i * nseg:(i + 1) * nseg, :]
        y_ref[:, k * cwid:(k + 1) * cwid] = us_ref[k]


def _s5_prompt(z, bw, cw, d, are, aim, batch, seq):
    nseg, seglen = SUBLANES, 64
    rows = nseg * seglen
    ntc = seq // rows
    gn = S5_GROUPS * S5_STATE
    nsb = S5_STEP_BLOCKS
    wspec = lambda shape: pl.BlockSpec((nsb,) + shape, lambda b, g, t: (g, 0, 0))
    vec = lambda w: pl.BlockSpec((1, nsb * w), lambda b, g, t: (0, g))
    st = pl.BlockSpec((None, 1, nsb * S5_SW), lambda b, g, t: (b, 0, g))
    slab = pltpu.VMEM((nsb, rows, S5_CW), F32)
    y, sre, sim = pl.pallas_call(
        functools.partial(_s5_kernel, nseg=nseg, seglen=seglen, chained=True, ntc=ntc),
        grid=(batch, S5_NBLK // nsb, ntc),
        in_specs=[
            pl.BlockSpec((rows, nsb * S5_CW), lambda b, g, t: (b * ntc + t, g)),
            wspec((S5_CW, 2 * S5_SW)),
            wspec((2 * S5_SW, S5_CW)),
            vec(S5_CW), vec(S5_SW), vec(S5_SW),
        ],
        out_specs=[pl.BlockSpec((rows, nsb * S5_CW), lambda b, g, t: (b * ntc + t, g)), st, st],
        out_shape=[jax.ShapeDtypeStruct((z.shape[0], S5_WIDTH), F32),
                   jax.ShapeDtypeStruct((batch, 1, gn), F32),
                   jax.ShapeDtypeStruct((batch, 1, gn), F32)],
        scratch_shapes=[slab, slab,
                        pltpu.VMEM((nsb, rows, 2 * S5_SW), F32),
                        slab,
                        pltpu.VMEM((nsb, SUBLANES, 2 * S5_SW), F32)],
        compiler_params=_params("parallel", "parallel", "arbitrary"),
        name="s5_prompt",
    )(z, bw, cw, d, are, aim)
    shape = (batch, S5_GROUPS, S5_STATE)
    return y, sre.reshape(shape), sim.reshape(shape)


def _s5_sample(z, y_all, bw, cw, d, are, aim, h0_re, h0_im, row0, nb, nt):
    rows = nb * nt
    gn = S5_GROUPS * S5_STATE
    rb = row0 // rows
    nsb = S5_STEP_BLOCKS
    wspec = lambda shape: pl.BlockSpec((nsb,) + shape, lambda g: (g, 0, 0))
    vec = lambda w: pl.BlockSpec((1, nsb * w), lambda g: (0, g))
    st = pl.BlockSpec((nb, nsb * S5_SW), lambda g: (0, g))
    slab = pltpu.VMEM((nsb, rows, S5_CW), F32)
    y, sre, sim = pl.pallas_call(
        functools.partial(_s5_kernel, nseg=nb, seglen=nt, chained=False, ntc=1),
        grid=(S5_NBLK // nsb,),
        in_specs=[
            pl.BlockSpec((rows, nsb * S5_CW), lambda g: (rb, g)),
            wspec((S5_CW, 2 * S5_SW)),
            wspec((2 * S5_SW, S5_CW)),
            vec(S5_CW), vec(S5_SW), vec(S5_SW), st, st,
            pl.BlockSpec(memory_space=pl.ANY),
        ],
        out_specs=[pl.BlockSpec((rows, nsb * S5_CW), lambda g: (rb, g)), st, st],
        out_shape=[jax.ShapeDtypeStruct(y_all.shape, F32),
                   jax.ShapeDtypeStruct((nb, gn), F32),
                   jax.ShapeDtypeStruct((nb, gn), F32)],
        input_output_aliases={8: 0},
        scratch_shapes=[slab, slab,
                        pltpu.VMEM((nsb, rows, 2 * S5_SW), F32),
                        slab],
        compiler_params=_params("parallel"),
        name="s5_sample",
    )(z, bw, cw, d, are, aim, h0_re.reshape(nb, gn), h0_im.reshape(nb, gn), y_all)
    shape = (nb, S5_GROUPS, S5_STATE)
    return y, sre.reshape(shape), sim.reshape(shape)


def _cumsum_rows(x, seq_len):
    ngroup = GLA_ROWS // SUBLANES
    y = x.reshape(ngroup, SUBLANES, LANES)
    r = lax.broadcasted_iota(jnp.int32, y.shape, 1)
    for d in (1, 2, 4):
        y = y + jnp.where(r >= d, pltpu.roll(y, d, axis=1), 0.0)
    if seq_len == SUBLANES:
        return y.reshape(GLA_ROWS, LANES)
    parts, acc = [], None
    for g in range(ngroup):
        if g % (seq_len // SUBLANES) == 0:
            acc = None
        part = y[g] if acc is None else y[g] + acc
        parts.append(part)
        acc = part[SUBLANES - 1:SUBLANES, :]
    return jnp.concatenate(parts, axis=0)


def _level_ref(b, level):
    half = 1 << level
    span = 2 * half
    if span >= SUBLANES:
        b3 = b.reshape(GLA_ROWS // span, span, LANES)
        return jnp.broadcast_to(b3[:, half - 1:half, :], b3.shape).reshape(GLA_ROWS, LANES)
    b3 = b.reshape(GLA_ROWS // SUBLANES, SUBLANES, LANES)
    r = lax.broadcasted_iota(jnp.int32, b3.shape, 1)
    nspan = SUBLANES // span
    ref = jnp.broadcast_to(b3[:, (nspan - 1) * span + half - 1:(nspan - 1) * span + half, :], b3.shape)
    for p in range(nspan - 2, -1, -1):
        piece = jnp.broadcast_to(b3[:, p * span + half - 1:p * span + half, :], b3.shape)
        ref = jnp.where(r < (p + 1) * span, piece, ref)
    return ref.reshape(GLA_ROWS, LANES)


def _gla_tables(seq_len):
    t = lax.broadcasted_iota(jnp.int32, (GLA_ROWS, GLA_ROWS), 0)
    s = lax.broadcasted_iota(jnp.int32, (GLA_ROWS, GLA_ROWS), 1)
    x = t ^ s
    level = jnp.zeros((GLA_ROWS, GLA_ROWS), jnp.int32)
    for k in range(1, 6):
        level = level + jnp.where(x >= (1 << k), 1, 0)
    level = jnp.where(s < t, level, jnp.where(s == t, GLA_DIAG, -1))
    rows = lax.broadcasted_iota(jnp.int32, (GLA_ROWS, LANES), 0)
    return level, rows


def _gla_block(q, fz, v, lb, states, tables, seq_len, transposed_state):
    level, rows = tables
    nseq = GLA_ROWS // seq_len
    nlev = seq_len.bit_length() - 1
    qc = jax.nn.silu(q)
    fg = lb + (1.0 - lb) * jax.nn.sigmoid(fz)
    kc = 1.0 - fg
    vb = v.astype(BF16)
    b = _cumsum_rows(jnp.log(fg), seq_len)

    nt = (((1,), (1,)), ((), ()))
    att = jnp.where(level == GLA_DIAG,
                    lax.dot_general(qc.astype(BF16), kc.astype(BF16), nt, preferred_element_type=F32),
                    0.0)
    for lev in range(nlev):
        w = jnp.exp(-jnp.abs(b - _level_ref(b, lev)))
        upper = ((rows >> lev) & 1) == 1
        qh = jnp.where(upper, qc * w, 0.0).astype(BF16)
        kh = jnp.where(upper, 0.0, kc * w).astype(BF16)
        att = jnp.where(level == lev,
                        lax.dot_general(qh, kh, nt, preferred_element_type=F32), att)
    o = _dot(att.astype(BF16), vb)

    b3 = b.reshape(nseq, seq_len, LANES)
    blast = jnp.broadcast_to(b3[:, seq_len - 1:seq_len, :], b3.shape).reshape(GLA_ROWS, LANES)
    qe = (qc * jnp.exp(b)).astype(BF16)
    kd = (kc * jnp.exp(blast - b)).astype(BF16)
    tn = (((0,), (0,)), ((), ()))
    o_inter, new_states = [], []
    for n in range(nseq):
        sl = slice(n * seq_len, (n + 1) * seq_len)
        s_n = states[n]
        e_row = jnp.exp(blast[n * seq_len:n * seq_len + 1, :])
        if transposed_state:
            o_inter.append(lax.dot_general(qe[sl], s_n.astype(BF16), nt, preferred_element_type=F32))
            upd = lax.dot_general(vb[sl], kd[sl], tn, preferred_element_type=F32)
            new_states.append(e_row * s_n + upd)
        else:
            o_inter.append(_dot(qe[sl], s_n.astype(BF16)))
            upd = lax.dot_general(kd[sl], vb[sl], tn, preferred_element_type=F32)
            e_col = jnp.transpose(jnp.broadcast_to(e_row, (HGRN_DK, LANES)))
            new_states.append(e_col * s_n + upd)
    o = o + (o_inter[0] if nseq == 1 else jnp.concatenate(o_inter, axis=0))
    return o, new_states


def _lower_bound(lbp, layer):
    if layer == 0:
        return jnp.zeros((1, lbp.shape[1]), F32)
    e = jnp.exp(lbp - jnp.max(lbp, axis=0, keepdims=True))
    p = e / jnp.sum(e, axis=0, keepdims=True)
    return jnp.sum(p[1:layer + 1, :], axis=0, keepdims=True)


def _gla_finish(o, g, gain):
    o = o * lax.rsqrt(jnp.mean(o * o, axis=-1, keepdims=True) + EPS)
    return o * gain * jax.nn.silu(g)


def _hgrn_prompt_kernel(q_ref, f_ref, v_ref, g_ref, lbp_ref, gain_ref, o_ref, sout_ref, s_ref,
                        *, layer, nchunk, ntb):
    tb = pl.program_id(2)

    @pl.when(tb == 0)
    def _():
        s_ref[...] = jnp.zeros_like(s_ref)

    tables = _gla_tables(GLA_ROWS)
    lb = _lower_bound(lbp_ref[...], layer)
    gain = gain_ref[layer:layer + 1, :]

    def chunk(c, _):
        r = pl.ds(pl.multiple_of(c * GLA_ROWS, GLA_ROWS), GLA_ROWS)
        for hh in range(HGRN_HB):
            cs = slice(hh * LANES, (hh + 1) * LANES)
            o, (s_new,) = _gla_block(q_ref[r, cs], f_ref[r, cs], v_ref[r, cs], lb[:, cs],
                                     [s_ref[hh]], tables, GLA_ROWS, True)
            s_ref[hh] = s_new
            o_ref[r, cs] = _gla_finish(o, g_ref[r, cs], gain[:, cs])
        return 0

    lax.fori_loop(0, nchunk, chunk, 0)

    @pl.when(tb == ntb - 1)
    def _():
        for hh in range(HGRN_HB):
            sout_ref[hh] = jnp.transpose(s_ref[hh])


def _hgrn_prompt(z, lbp, gain, layer, batch, seq):
    tb_rows = 512
    ntb = seq // tb_rows
    wb = HGRN_HB * LANES
    nhb = HGRN_HEADS // HGRN_HB

    def zspec(k):
        c0 = (S5_WIDTH + k * HGRN_WIDTH) // wb
        return pl.BlockSpec((tb_rows, wb), lambda b, h, t: (b * ntb + t, c0 + h))

    par = pl.BlockSpec((lbp.shape[0], wb), lambda b, h, t: (0, h))
    return pl.pallas_call(
        functools.partial(_hgrn_prompt_kernel, layer=layer, nchunk=tb_rows // GLA_ROWS, ntb=ntb),
        grid=(batch, nhb, ntb),
        in_specs=[zspec(0), zspec(1), zspec(2), zspec(3), par, par],
        out_specs=[pl.BlockSpec((tb_rows, wb), lambda b, h, t: (b * ntb + t, h)),
                   pl.BlockSpec((None, HGRN_HB, HGRN_DK, HGRN_DV), lambda b, h, t: (b, h, 0, 0))],
        out_shape=[jax.ShapeDtypeStruct((z.shape[0], HGRN_WIDTH), F32),
                   jax.ShapeDtypeStruct((batch, HGRN_HEADS, HGRN_DK, HGRN_DV), F32)],
        scratch_shapes=[pltpu.VMEM((HGRN_HB, HGRN_DK, HGRN_DV), F32)],
        compiler_params=_params("parallel", "parallel", "arbitrary"),
        name="hgrn_prompt",
    )(z, z, z, z, lbp, gain)


def _hgrn_sample_kernel(q_ref, f_ref, v_ref, g_ref, lbp_ref, gain_ref, s0_ref, *rest, layer, nt):
    o_ref, sout_ref = rest[-2:]
    nseq = GLA_ROWS // nt
    tables = _gla_tables(nt)
    lb = _lower_bound(lbp_ref[...], layer)
    gain = gain_ref[layer:layer + 1, :]
    for hh in range(HGRN_HB):
        cs = slice(hh * LANES, (hh + 1) * LANES)
        o, new_states = _gla_block(q_ref[:, cs], f_ref[:, cs], v_ref[:, cs], lb[:, cs],
                                   [s0_ref[n, hh] for n in range(nseq)], tables, nt, False)
        for n in range(nseq):
            sout_ref[n, hh] = new_states[n].astype(sout_ref.dtype)
        o_ref[:, cs] = _gla_finish(o, g_ref[:, cs], gain[:, cs])


def _hgrn_sample(z, y_all, lbp, gain, s0, s_all, layer, row0, nb, nt):
    nseq = GLA_ROWS // nt
    rb0 = row0 // GLA_ROWS
    wb = HGRN_HB * LANES

    def zspec(k):
        c0 = (S5_WIDTH + k * HGRN_WIDTH) // wb
        return pl.BlockSpec((GLA_ROWS, wb), lambda h, i: (rb0 + i, c0 + h))

    par = pl.BlockSpec((lbp.shape[0], wb), lambda h, i: (0, h))
    sspec = pl.BlockSpec((None, nseq, HGRN_HB, HGRN_DK, HGRN_DV), lambda h, i: (layer, i, h, 0, 0))
    anyspec = pl.BlockSpec(memory_space=pl.ANY)
    carried = (y_all,) if s_all is None else (y_all, s_all)
    aliases = {7: 0} if s_all is None else {7: 0, 8: 1}
    return pl.pallas_call(
        functools.partial(_hgrn_sample_kernel, layer=layer, nt=nt),
        grid=(HGRN_HEADS // HGRN_HB, nb // nseq),
        in_specs=[zspec(0), zspec(1), zspec(2), zspec(3), par, par, sspec] + [anyspec] * len(carried),
        out_specs=[pl.BlockSpec((GLA_ROWS, wb), lambda h, i: (rb0 + i, h)), sspec],
        out_shape=[jax.ShapeDtypeStruct(y_all.shape, F32),
                   jax.ShapeDtypeStruct(s0.shape, s0.dtype)],
        input_output_aliases=aliases,
        compiler_params=_params("parallel", "parallel"),
        name="hgrn_sample",
    )(z, z, z, z, lbp, gain, s0, *carried)


def _mixout_kernel(h_ref, ys_ref, yh_ref, wglu_ref, bglu_ref, wo_ref, gpost_ref, o_ref):
    ys = ys_ref[...]
    gate = jax.nn.sigmoid(_dot(ys.astype(BF16), wglu_ref[...]) + bglu_ref[...])
    out = (_dot((ys * gate).astype(BF16), wo_ref[:S5_WIDTH, :])
           + _dot(yh_ref[...].astype(BF16), wo_ref[S5_WIDTH:, :]))
    o_ref[...] = h_ref[...] + _rms(out, gpost_ref[...])


def _mixout(h, ys, yh, w_glu, b_glu, w_out, gpost):
    m = h.shape[0]
    tm = _row_tile(m)
    full = lambda shape: pl.BlockSpec(shape, lambda i: (0, 0))
    return pl.pallas_call(
        _mixout_kernel,
        grid=(m // tm,),
        in_specs=[
            pl.BlockSpec((tm, D_MODEL), lambda i: (i, 0)),
            pl.BlockSpec((tm, S5_WIDTH), lambda i: (i, 0)),
            pl.BlockSpec((tm, HGRN_WIDTH), lambda i: (i, 0)),
            full((S5_WIDTH, S5_WIDTH)), full((1, S5_WIDTH)),
            full((D_MODEL, D_MODEL)), full((1, D_MODEL)),
        ],
        out_specs=pl.BlockSpec((tm, D_MODEL), lambda i: (i, 0)),
        out_shape=jax.ShapeDtypeStruct((m, D_MODEL), F32),
        compiler_params=_params("parallel"),
        name="mixout",
    )(h, ys, yh, w_glu, b_glu, w_out, gpost)


def kernel(x_prompt, x_sample, state_s5_re, state_s5_im, state_hgrn, norm_pre, norm_post, ffn1_w_gate, ffn1_w_up, ffn1_w_down, ffn2_w_gate, ffn2_w_up, ffn2_w_down, w_in, w_out, s5_lam_re, s5_lam_im, s5_log_dt, s5_b_re, s5_b_im, s5_c_re, s5_c_im, s5_d, s5_w_glu, s5_b_glu, hgrn_lb, hgrn_norm):
    batch, seq, _ = x_prompt.shape
    nb, nt, _ = x_sample.shape
    depth = norm_pre.shape[0]
    mp = batch * seq
    sdt = state_hgrn.dtype

    h = jnp.concatenate([x_prompt.reshape(mp, D_MODEL), x_sample.reshape(nb * nt, D_MODEL)], axis=0)

    are, aim, bb_re, bb_im = _s5_discretise(s5_lam_re, s5_lam_im, s5_log_dt, s5_b_re, s5_b_im)
    bw, cw = _s5_block_weights(bb_re, bb_im, s5_c_re, s5_c_im)

    vec = lambda a: a.reshape(1, -1)

    small = ("re_p", "im_p", "h_p", "re_s", "im_s")
    outs = {k: [] for k in small}
    h_s_all = None
    for l in range(depth):
        h = _ffn(h, vec(norm_pre[l, 0]), vec(norm_post[l, 0]),
                 ffn1_w_gate, ffn1_w_up, ffn1_w_down, l)

        z = _mixin(h, vec(norm_pre[l, 1]), w_in, l)
        d = vec(s5_d[l])
        ys, re_p, im_p = _s5_prompt(z, bw[l], cw[l], d, are[l], aim[l], batch, seq)
        ys, re_s, im_s = _s5_sample(z, ys, bw[l], cw[l], d, are[l], aim[l],
                                    state_s5_re[l], state_s5_im[l], mp, nb, nt)
        yh, h_p = _hgrn_prompt(z, hgrn_lb, hgrn_norm, l, batch, seq)
        yh, h_s_all = _hgrn_sample(z, yh, hgrn_lb, hgrn_norm, state_hgrn, h_s_all, l, mp, nb, nt)
        h = _mixout(h, ys, yh, s5_w_glu[l].astype(BF16), vec(s5_b_glu[l]), w_out[l].astype(BF16),
                    vec(norm_post[l, 1]))

        h = _ffn(h, vec(norm_pre[l, 2]), vec(norm_post[l, 2]),
                 ffn2_w_gate, ffn2_w_up, ffn2_w_down, l)
        for k, a in zip(small, (re_p, im_p, h_p, re_s, im_s)):
            outs[k].append(a)

    stack = lambda k: jnp.stack(outs[k]).astype(sdt)
    return (h[:mp].reshape(batch, seq, D_MODEL), h[mp:].reshape(nb, nt, D_MODEL),
            stack("re_p"), stack("im_p"), stack("h_p"),
            stack("re_s"), stack("im_s"), h_s_all)
```

```python
import functools

import jax
import jax.numpy as jnp
from jax import lax
from jax.experimental import pallas as pl
from jax.experimental.pallas import tpu as pltpu

F32 = jnp.float32
BF16 = jnp.bfloat16

D_MODEL = 2048
S5_WIDTH = 1024
S5_GROUP = 16
S5_GROUPS = 64
S5_STATE = 64
HGRN_WIDTH = 1024
HGRN_DK = 128
HGRN_DV = 128
HGRN_HEADS = 8
D_FF = 5504
IN_WIDTH = S5_WIDTH + 4 * HGRN_WIDTH
EPS = 1e-6

LANES = 128
SUBLANES = 8
VMEM_LIMIT_BYTES = 60 * 1024 * 1024

TOKEN_TILE = 1024
FF_TILE = 256
IN_TILE = 1024
NORM_ROWS = 256
HGRN_HB = 8
S5_GB = 8
S5_NBLK = S5_GROUPS // S5_GB
S5_STEP_BLOCKS = 2
S5_SW = S5_GB * S5_STATE
S5_CW = S5_GB * S5_GROUP
GLA_ROWS = 64
GLA_DIAG = 7


def _params(*sem):
    return pltpu.CompilerParams(dimension_semantics=sem, vmem_limit_bytes=VMEM_LIMIT_BYTES)


def _row_tile(m, largest=512):
    for t in (1024, 512, 256, 128, 64):
        if t <= largest and m % t == 0:
            return t
    raise ValueError(f"token count {m} must be a multiple of 64")


def _rms(x, gain):
    return x * lax.rsqrt(jnp.mean(x * x, axis=-1, keepdims=True) + EPS) * gain


def _dot(a, b):
    return jnp.dot(a, b, preferred_element_type=F32)


def _for_row_chunks(nrows, fn):
    step = min(NORM_ROWS, nrows)
    for c in range(nrows // step):
        fn(pl.ds(c * step, step))


def _ffn_kernel(x_ref, gpre_ref, gpost_ref, wg_ref, wu_ref, wd_ref, o_ref, xn_ref, *, nj):
    j = pl.program_id(1)
    nrows = x_ref.shape[0]
    nvalid_last = D_FF - (nj - 1) * FF_TILE

    def weights(last):
        wd = wd_ref[...]
        if last:
            row = lax.broadcasted_iota(jnp.int32, wd.shape, 0)
            wd = jnp.where(row < nvalid_last, wd, 0.0)
        return wg_ref[...].astype(BF16), wu_ref[...].astype(BF16), wd.astype(BF16)

    def swiglu(r, w, last):
        xn = xn_ref[r, :]
        a = jax.nn.silu(_dot(xn, w[0])) * _dot(xn, w[1])
        if last:
            col = lax.broadcasted_iota(jnp.int32, a.shape, 1)
            a = jnp.where(col < nvalid_last, a, 0.0)
        return _dot(a.astype(BF16), w[2])

    @pl.when(j == 0)
    def _():
        w = weights(False)

        def first(r):
            xn_ref[r, :] = _rms(x_ref[r, :], gpre_ref[...]).astype(BF16)
            o_ref[r, :] = swiglu(r, w, False)

        _for_row_chunks(nrows, first)

    @pl.when((j > 0) & (j < nj - 1))
    def _():
        o_ref[...] += swiglu(pl.ds(0, nrows), weights(False), False)

    @pl.when(j == nj - 1)
    def _():
        w = weights(True)

        def last(r):
            acc = o_ref[r, :] + swiglu(r, w, True)
            o_ref[r, :] = x_ref[r, :] + 0.5 * _rms(acc, gpost_ref[...])

        _for_row_chunks(nrows, last)


def _ffn(h, gpre, gpost, wg, wu, wd, layer):
    m = h.shape[0]
    tm = _row_tile(m, TOKEN_TILE)
    nj = pl.cdiv(D_FF, FF_TILE)
    assert nj >= 2
    return pl.pallas_call(
        functools.partial(_ffn_kernel, nj=nj),
        grid=(m // tm, nj),
        in_specs=[
            pl.BlockSpec((tm, D_MODEL), lambda i, j: (i, 0)),
            pl.BlockSpec((1, D_MODEL), lambda i, j: (0, 0)),
            pl.BlockSpec((1, D_MODEL), lambda i, j: (0, 0)),
            pl.BlockSpec((None, D_MODEL, FF_TILE), lambda i, j: (layer, 0, j)),
            pl.BlockSpec((None, D_MODEL, FF_TILE), lambda i, j: (layer, 0, j)),
            pl.BlockSpec((None, FF_TILE, D_MODEL), lambda i, j: (layer, j, 0)),
        ],
        out_specs=pl.BlockSpec((tm, D_MODEL), lambda i, j: (i, 0)),
        out_shape=jax.ShapeDtypeStruct((m, D_MODEL), F32),
        scratch_shapes=[pltpu.VMEM((tm, D_MODEL), BF16)],
        compiler_params=_params("parallel", "arbitrary"),
        name="ffn",
    )(h, gpre, gpost, wg, wu, wd)


def _mixin_kernel(x_ref, gpre_ref, w_ref, o_ref, xn_ref):
    @pl.when(pl.program_id(1) == 0)
    def _():
        def norm_in(r):
            xn_ref[r, :] = _rms(x_ref[r, :], gpre_ref[...]).astype(BF16)

        _for_row_chunks(x_ref.shape[0], norm_in)

    o_ref[...] = _dot(xn_ref[...], w_ref[...].astype(BF16))


def _mixin(h, gpre, w_in, layer):
    m = h.shape[0]
    tm = _row_tile(m, TOKEN_TILE)
    return pl.pallas_call(
        _mixin_kernel,
        grid=(m // tm, IN_WIDTH // IN_TILE),
        in_specs=[
            pl.BlockSpec((tm, D_MODEL), lambda i, j: (i, 0)),
            pl.BlockSpec((1, D_MODEL), lambda i, j: (0, 0)),
            pl.BlockSpec((None, D_MODEL, IN_TILE), lambda i, j: (layer, 0, j)),
        ],
        out_specs=pl.BlockSpec((tm, IN_TILE), lambda i, j: (i, j)),
        out_shape=jax.ShapeDtypeStruct((m, IN_WIDTH), F32),
        scratch_shapes=[pltpu.VMEM((tm, D_MODEL), BF16)],
        compiler_params=_params("parallel", "arbitrary"),
        name="mixin",
    )(h, gpre, w_in)


def _s5_disc_kernel(lre_ref, lim_ref, ldt_ref, bre_ref, bim_ref,
                    are_ref, aim_ref, bbre_ref, bbim_ref):
    lam_re = lre_ref[...]
    lam_im = lim_ref[...]
    dt = jnp.exp(ldt_ref[...])
    mag = jnp.exp(lam_re * dt)
    ang = lam_im * dt
    abar_re = mag * jnp.cos(ang)
    abar_im = mag * jnp.sin(ang)
    p = abar_re - 1.0
    den = lam_re * lam_re + lam_im * lam_im
    z_re = (p * lam_re + abar_im * lam_im) / den
    z_im = (abar_im * lam_re - p * lam_im) / den
    are_ref[...] = abar_re
    aim_ref[...] = abar_im
    b_re = bre_ref[...]
    b_im = bim_ref[...]
    bbre_ref[...] = z_re * b_re - z_im * b_im
    bbim_ref[...] = z_re * b_im + z_im * b_re


def _s5_discretise(lam_re, lam_im, log_dt, b_re, b_im):
    depth = lam_re.shape[0]
    gn = S5_GROUPS * S5_STATE
    flat = lambda a: a.reshape(depth, 1, gn)
    ldt = jnp.broadcast_to(log_dt[:, :, None], lam_re.shape)
    chan_major = lambda b: jnp.transpose(b, (0, 3, 1, 2)).reshape(depth, S5_GROUP, gn)
    row = pl.BlockSpec((None, 1, gn), lambda l: (l, 0, 0))
    mat = pl.BlockSpec((None, S5_GROUP, gn), lambda l: (l, 0, 0))
    return pl.pallas_call(
        _s5_disc_kernel,
        grid=(depth,),
        in_specs=[row, row, row, mat, mat],
        out_specs=[row, row, mat, mat],
        out_shape=[jax.ShapeDtypeStruct((depth, 1, gn), F32)] * 2
        + [jax.ShapeDtypeStruct((depth, S5_GROUP, gn), F32)] * 2,
        compiler_params=_params("parallel"),
        name="s5_discretise",
    )(flat(lam_re), flat(lam_im), flat(ldt), chan_major(b_re), chan_major(b_im))


def _s5_block_weights(bb_re, bb_im, c_re, c_im):
    depth = bb_re.shape[0]
    eye = jnp.eye(S5_GB, dtype=F32)

    def in_map(bb):
        bb = bb.reshape(depth, S5_GROUP, S5_NBLK, S5_GB, S5_STATE)
        w = jnp.einsum("lcbgn,gh->lbgchn", bb, eye)
        return w.reshape(depth, S5_NBLK, S5_CW, S5_SW)

    def out_map(c):
        c = c.reshape(depth, S5_NBLK, S5_GB, S5_GROUP, S5_STATE)
        w = jnp.einsum("lbgcn,gh->lbgnhc", c, eye)
        return w.reshape(depth, S5_NBLK, S5_SW, S5_CW)

    bw = jnp.concatenate([in_map(bb_re), in_map(bb_im)], axis=-1).astype(BF16)
    cw = jnp.concatenate([out_map(c_re), -out_map(c_im)], axis=-2).astype(BF16)
    return bw, cw


def _s5_kernel(*refs, nseg, seglen, chained, ntc):
    if chained:
        (u_ref, bw_ref, cw_ref, d_ref, are_ref, aim_ref,
         y_ref, sre_ref, sim_ref, us_ref, up_ref, x_ref, yp_ref, carry_ref) = refs
    else:
        (u_ref, bw_ref, cw_ref, d_ref, are_ref, aim_ref, h0re_ref, h0im_ref, y_all_ref,
         y_ref, sre_ref, sim_ref, us_ref, up_ref, x_ref, yp_ref) = refs
    sw, cwid = S5_SW, S5_CW
    blocks = range(S5_STEP_BLOCKS)

    for k in blocks:
        us_ref[k] = u_ref[:, k * cwid:(k + 1) * cwid]
    for k in blocks:
        for i in range(seglen):
            up_ref[k, i * nseg:(i + 1) * nseg, :] = us_ref.at[k][pl.ds(i, nseg, stride=seglen), :]
        x_ref[k] = _dot(up_ref[k].astype(BF16), bw_ref[k])

    abar = []
    for k in blocks:
        ar = are_ref[:, k * sw:(k + 1) * sw]
        ai = aim_ref[:, k * sw:(k + 1) * sw]
        abar.append((ar, ai, jnp.broadcast_to(ar, (SUBLANES, sw)), jnp.broadcast_to(ai, (SUBLANES, sw))))

    def step(k, xr, xi, r0):
        _, _, arb, aib = abar[k]
        br = x_ref[k, pl.ds(r0, SUBLANES), :sw]
        bi = x_ref[k, pl.ds(r0, SUBLANES), sw:]
        return arb * xr - aib * xi + br, arb * xi + aib * xr + bi

    def step_store(k, xr, xi, r0):
        xr, xi = step(k, xr, xi, r0)
        x_ref[k, pl.ds(r0, SUBLANES), :sw] = xr
        x_ref[k, pl.ds(r0, SUBLANES), sw:] = xi
        return xr, xi

    if chained:
        tc = pl.program_id(2)

        @pl.when(tc == 0)
        def _():
            carry_ref[...] = jnp.zeros_like(carry_ref)

        for k in blocks:
            ar, ai, _, _ = abar[k]
            er = ei = jnp.zeros((SUBLANES, sw), F32)
            for i in range(seglen):
                er, ei = step(k, er, ei, i * SUBLANES)
            pr, pi = ar, ai
            for _ in range(seglen.bit_length() - 1):
                pr, pi = pr * pr - pi * pi, 2.0 * pr * pi
            cr = carry_ref[k, 0:1, :sw]
            ci = carry_ref[k, 0:1, sw:]
            starts_r, starts_i = [], []
            for j in range(SUBLANES):
                starts_r.append(cr)
                starts_i.append(ci)
                cr, ci = (pr * cr - pi * ci + er[j:j + 1, :],
                          pr * ci + pi * cr + ei[j:j + 1, :])
            carry_ref[k, 0:1, :sw] = cr
            carry_ref[k, 0:1, sw:] = ci
            xr = jnp.concatenate(starts_r, axis=0)
            xi = jnp.concatenate(starts_i, axis=0)
            for i in range(seglen):
                xr, xi = step_store(k, xr, xi, i * SUBLANES)

        @pl.when(tc == ntc - 1)
        def _():
            for k in blocks:
                sre_ref[:, k * sw:(k + 1) * sw] = carry_ref[k, 0:1, :sw]
                sim_ref[:, k * sw:(k + 1) * sw] = carry_ref[k, 0:1, sw:]
    else:
        def scan_group(sg, _):
            g0 = pl.multiple_of(sg * SUBLANES, SUBLANES)
            for k in blocks:
                xr = h0re_ref[pl.ds(g0, SUBLANES), k * sw:(k + 1) * sw]
                xi = h0im_ref[pl.ds(g0, SUBLANES), k * sw:(k + 1) * sw]
                for i in range(seglen):
                    xr, xi = step_store(k, xr, xi, pl.multiple_of(i * nseg + g0, SUBLANES))
                sre_ref[pl.ds(g0, SUBLANES), k * sw:(k + 1) * sw] = xr
                sim_ref[pl.ds(g0, SUBLANES), k * sw:(k + 1) * sw] = xi
            return 0

        lax.fori_loop(0, nseg // SUBLANES, scan_group, 0)

    for k in blocks:
        y = (_dot(x_ref[k].astype(BF16), cw_ref[k])
             + d_ref[:, k * cwid:(k + 1) * cwid] * up_ref[k])
        yp_ref[k] = jax.nn.gelu(y)
        for i in range(seglen):
            us_ref.at[k][pl.ds(i, nseg, stride=seglen), :] = yp_ref[k, i * nseg:(i + 1) * nseg, :]
        y_ref[:, k * cwid:(k + 1) * cwid] = us_ref[k]


def _s5_prompt(z, bw, cw, d, are, aim, batch, seq):
    nseg, seglen = SUBLANES, 64
    rows = nseg * seglen
    ntc = seq // rows
    gn = S5_GROUPS * S5_STATE
    nsb = S5_STEP_BLOCKS
    wspec = lambda shape: pl.BlockSpec((nsb,) + shape, lambda b, g, t: (g, 0, 0))
    vec = lambda w: pl.BlockSpec((1, nsb * w), lambda b, g, t: (0, g))
    st = pl.BlockSpec((None, 1, nsb * S5_SW), lambda b, g, t: (b, 0, g))
    slab = pltpu.VMEM((nsb, rows, S5_CW), F32)
    y, sre, sim = pl.pallas_call(
        functools.partial(_s5_kernel, nseg=nseg, seglen=seglen, chained=True, ntc=ntc),
        grid=(batch, S5_NBLK // nsb, ntc),
        in_specs=[
            pl.BlockSpec((rows, nsb * S5_CW), lambda b, g, t: (b * ntc + t, g)),
            wspec((S5_CW, 2 * S5_SW)),
            wspec((2 * S5_SW, S5_CW)),
            vec(S5_CW), vec(S5_SW), vec(S5_SW),
        ],
        out_specs=[pl.BlockSpec((rows, nsb * S5_CW), lambda b, g, t: (b * ntc + t, g)), st, st],
        out_shape=[jax.ShapeDtypeStruct((z.shape[0], S5_WIDTH), F32),
                   jax.ShapeDtypeStruct((batch, 1, gn), F32),
                   jax.ShapeDtypeStruct((batch, 1, gn), F32)],
        scratch_shapes=[slab, slab,
                        pltpu.VMEM((nsb, rows, 2 * S5_SW), F32),
                        slab,
                        pltpu.VMEM((nsb, SUBLANES, 2 * S5_SW), F32)],
        compiler_params=_params("parallel", "parallel", "arbitrary"),
        name="s5_prompt",
    )(z, bw, cw, d, are, aim)
    shape = (batch, S5_GROUPS, S5_STATE)
    return y, sre.reshape(shape), sim.reshape(shape)


def _s5_sample(z, y_all, bw, cw, d, are, aim, h0_re, h0_im, row0, nb, nt):
    rows = nb * nt
    gn = S5_GROUPS * S5_STATE
    rb = row0 // rows
    nsb = S5_STEP_BLOCKS
    wspec = lambda shape: pl.BlockSpec((nsb,) + shape, lambda g: (g, 0, 0))
    vec = lambda w: pl.BlockSpec((1, nsb * w), lambda g: (0, g))
    st = pl.BlockSpec((nb, nsb * S5_SW), lambda g: (0, g))
    slab = pltpu.VMEM((nsb, rows, S5_CW), F32)
    y, sre, sim = pl.pallas_call(
        functools.partial(_s5_kernel, nseg=nb, seglen=nt, chained=False, ntc=1),
        grid=(S5_NBLK // nsb,),
        in_specs=[
            pl.BlockSpec((rows, nsb * S5_CW), lambda g: (rb, g)),
            wspec((S5_CW, 2 * S5_SW)),
            wspec((2 * S5_SW, S5_CW)),
            vec(S5_CW), vec(S5_SW), vec(S5_SW), st, st,
            pl.BlockSpec(memory_space=pl.ANY),
        ],
        out_specs=[pl.BlockSpec((rows, nsb * S5_CW), lambda g: (rb, g)), st, st],
        out_shape=[jax.ShapeDtypeStruct(y_all.shape, F32),
                   jax.ShapeDtypeStruct((nb, gn), F32),
                   jax.ShapeDtypeStruct((nb, gn), F32)],
        input_output_aliases={8: 0},
        scratch_shapes=[slab, slab,
                        pltpu.VMEM((nsb, rows, 2 * S5_SW), F32),
                        slab],
        compiler_params=_params("parallel"),
        name="s5_sample",
    )(z, bw, cw, d, are, aim, h0_re.reshape(nb, gn), h0_im.reshape(nb, gn), y_all)
    shape = (nb, S5_GROUPS, S5_STATE)
    return y, sre.reshape(shape), sim.reshape(shape)


def _cumsum_rows(x, seq_len):
    ngroup = GLA_ROWS // SUBLANES
    y = x.reshape(ngroup, SUBLANES, LANES)
    r = lax.broadcasted_iota(jnp.int32, y.shape, 1)
    for d in (1, 2, 4):
        y = y + jnp.where(r >= d, pltpu.roll(y, d, axis=1), 0.0)
    if seq_len == SUBLANES:
        return y.reshape(GLA_ROWS, LANES)
    parts, acc = [], None
    for g in range(ngroup):
        if g % (seq_len // SUBLANES) == 0:
            acc = None
        part = y[g] if acc is None else y[g] + acc
        parts.append(part)
        acc = part[SUBLANES - 1:SUBLANES, :]
    return jnp.concatenate(parts, axis=0)


def _level_ref(b, level):
    half = 1 << level
    span = 2 * half
    if span >= SUBLANES:
        b3 = b.reshape(GLA_ROWS // span, span, LANES)
        return jnp.broadcast_to(b3[:, half - 1:half, :], b3.shape).reshape(GLA_ROWS, LANES)
    b3 = b.reshape(GLA_ROWS // SUBLANES, SUBLANES, LANES)
    r = lax.broadcasted_iota(jnp.int32, b3.shape, 1)
    nspan = SUBLANES // span
    ref = jnp.broadcast_to(b3[:, (nspan - 1) * span + half - 1:(nspan - 1) * span + half, :], b3.shape)
    for p in range(nspan - 2, -1, -1):
        piece = jnp.broadcast_to(b3[:, p * span + half - 1:p * span + half, :], b3.shape)
        ref = jnp.where(r < (p + 1) * span, piece, ref)
    return ref.reshape(GLA_ROWS, LANES)


def _gla_tables(seq_len):
    t = lax.broadcasted_iota(jnp.int32, (GLA_ROWS, GLA_ROWS), 0)
    s = lax.broadcasted_iota(jnp.int32, (GLA_ROWS, GLA_ROWS), 1)
    x = t ^ s
    level = jnp.zeros((GLA_ROWS, GLA_ROWS), jnp.int32)
    for k in range(1, 6):
        level = level + jnp.where(x >= (1 << k), 1, 0)
    level = jnp.where(s < t, level, jnp.where(s == t, GLA_DIAG, -1))
    rows = lax.broadcasted_iota(jnp.int32, (GLA_ROWS, LANES), 0)
    return level, rows


def _gla_block(q, fz, v, lb, states, tables, seq_len, transposed_state):
    level, rows = tables
    nseq = GLA_ROWS // seq_len
    nlev = seq_len.bit_length() - 1
    qc = jax.nn.silu(q)
    fg = lb + (1.0 - lb) * jax.nn.sigmoid(fz)
    kc = 1.0 - fg
    vb = v.astype(BF16)
    b = _cumsum_rows(jnp.log2(fg), seq_len)

    nt = (((1,), (1,)), ((), ()))
    att = jnp.where(level == GLA_DIAG,
                    lax.dot_general(qc.astype(BF16), kc.astype(BF16), nt, preferred_element_type=F32),
                    0.0)
    for lev in range(nlev):
        half = 1 << lev
        if half >= SUBLANES:
            shape3 = (GLA_ROWS // (2 * half), 2 * half, LANES)
            b3, q3, k3 = b.reshape(shape3), qc.reshape(shape3), kc.reshape(shape3)
            ref = b3[:, half - 1:half, :]
            zero = jnp.zeros((shape3[0], half, LANES), F32)
            qh = jnp.concatenate([zero, q3[:, half:, :] * jnp.exp2(b3[:, half:, :] - ref)], axis=1)
            kh = jnp.concatenate([k3[:, :half, :] * jnp.exp2(ref - b3[:, :half, :]), zero], axis=1)
            qh = qh.reshape(GLA_ROWS, LANES).astype(BF16)
            kh = kh.reshape(GLA_ROWS, LANES).astype(BF16)
        else:
            w = jnp.exp2(-jnp.abs(b - _level_ref(b, lev)))
            upper = ((rows >> lev) & 1) == 1
            qh = jnp.where(upper, qc * w, 0.0).astype(BF16)
            kh = jnp.where(upper, 0.0, kc * w).astype(BF16)
        att = jnp.where(level == lev,
                        lax.dot_general(qh, kh, nt, preferred_element_type=F32), att)
    o = _dot(att.astype(BF16), vb)

    b3 = b.reshape(nseq, seq_len, LANES)
    blast = jnp.broadcast_to(b3[:, seq_len - 1:seq_len, :], b3.shape).reshape(GLA_ROWS, LANES)
    qe = (qc * jnp.exp2(b)).astype(BF16)
    kd = (kc * jnp.exp2(blast - b)).astype(BF16)
    tn = (((0,), (0,)), ((), ()))
    o_inter, new_states = [], []
    for n in range(nseq):
        sl = slice(n * seq_len, (n + 1) * seq_len)
        s_n = states[n]
        e_row = jnp.exp2(blast[n * seq_len:n * seq_len + 1, :])
        if transposed_state:
            o_inter.append(lax.dot_general(qe[sl], s_n.astype(BF16), nt, preferred_element_type=F32))
            upd = lax.dot_general(vb[sl], kd[sl], tn, preferred_element_type=F32)
            new_states.append(e_row * s_n + upd)
        else:
            o_inter.append(_dot(qe[sl], s_n.astype(BF16)))
            upd = lax.dot_general(kd[sl], vb[sl], tn, preferred_element_type=F32)
            e_col = jnp.transpose(jnp.broadcast_to(e_row, (HGRN_DK, LANES)))
            new_states.append(e_col * s_n + upd)
    o = o + (o_inter[0] if nseq == 1 else jnp.concatenate(o_inter, axis=0))
    return o, new_states


def _lower_bound(lbp, layer):
    if layer == 0:
        return jnp.zeros((1, lbp.shape[1]), F32)
    e = jnp.exp(lbp - jnp.max(lbp, axis=0, keepdims=True))
    p = e / jnp.sum(e, axis=0, keepdims=True)
    return jnp.sum(p[1:layer + 1, :], axis=0, keepdims=True)


def _gla_finish(o, g, gain):
    o = o * lax.rsqrt(jnp.mean(o * o, axis=-1, keepdims=True) + EPS)
    return o * gain * jax.nn.silu(g)


def _hgrn_prompt_kernel(q_ref, f_ref, v_ref, g_ref, lbp_ref, gain_ref, o_ref, sout_ref, s_ref,
                        *, layer, nchunk, ntb):
    tb = pl.program_id(2)

    @pl.when(tb == 0)
    def _():
        s_ref[...] = jnp.zeros_like(s_ref)

    tables = _gla_tables(GLA_ROWS)
    lb = _lower_bound(lbp_ref[...], layer)
    gain = gain_ref[layer:layer + 1, :]

    def chunk(c, _):
        r = pl.ds(pl.multiple_of(c * GLA_ROWS, GLA_ROWS), GLA_ROWS)
        for hh in range(HGRN_HB):
            cs = slice(hh * LANES, (hh + 1) * LANES)
            o, (s_new,) = _gla_block(q_ref[r, cs], f_ref[r, cs], v_ref[r, cs], lb[:, cs],
                                     [s_ref[hh]], tables, GLA_ROWS, True)
            s_ref[hh] = s_new
            o_ref[r, cs] = _gla_finish(o, g_ref[r, cs], gain[:, cs])
        return 0

    lax.fori_loop(0, nchunk, chunk, 0)

    @pl.when(tb == ntb - 1)
    def _():
        for hh in range(HGRN_HB):
            sout_ref[hh] = jnp.transpose(s_ref[hh])


def _hgrn_prompt(z, lbp, gain, layer, batch, seq):
    tb_rows = 512
    ntb = seq // tb_rows
    wb = HGRN_HB * LANES
    nhb = HGRN_HEADS // HGRN_HB

    def zspec(k):
        c0 = (S5_WIDTH + k * HGRN_WIDTH) // wb
        return pl.BlockSpec((tb_rows, wb), lambda b, h, t: (b * ntb + t, c0 + h))

    par = pl.BlockSpec((lbp.shape[0], wb), lambda b, h, t: (0, h))
    return pl.pallas_call(
        functools.partial(_hgrn_prompt_kernel, layer=layer, nchunk=tb_rows // GLA_ROWS, ntb=ntb),
        grid=(batch, nhb, ntb),
        in_specs=[zspec(0), zspec(1), zspec(2), zspec(3), par, par],
        out_specs=[pl.BlockSpec((tb_rows, wb), lambda b, h, t: (b * ntb + t, h)),
                   pl.BlockSpec((None, HGRN_HB, HGRN_DK, HGRN_DV), lambda b, h, t: (b, h, 0, 0))],
        out_shape=[jax.ShapeDtypeStruct((z.shape[0], HGRN_WIDTH), F32),
                   jax.ShapeDtypeStruct((batch, HGRN_HEADS, HGRN_DK, HGRN_DV), F32)],
        scratch_shapes=[pltpu.VMEM((HGRN_HB, HGRN_DK, HGRN_DV), F32)],
        compiler_params=_params("parallel", "parallel", "arbitrary"),
        name="hgrn_prompt",
    )(z, z, z, z, lbp, gain)


def _hgrn_sample_kernel(q_ref, f_ref, v_ref, g_ref, lbp_ref, gain_ref, s0_ref, *rest, layer, nt):
    o_ref, sout_ref = rest[-2:]
    nseq = GLA_ROWS // nt
    tables = _gla_tables(nt)
    lb = _lower_bound(lbp_ref[...], layer)
    gain = gain_ref[layer:layer + 1, :]
    for hh in range(HGRN_HB):
        cs = slice(hh * LANES, (hh + 1) * LANES)
        o, new_states = _gla_block(q_ref[:, cs], f_ref[:, cs], v_ref[:, cs], lb[:, cs],
                                   [s0_ref[n, hh] for n in range(nseq)], tables, nt, False)
        for n in range(nseq):
            sout_ref[n, hh] = new_states[n].astype(sout_ref.dtype)
        o_ref[:, cs] = _gla_finish(o, g_ref[:, cs], gain[:, cs])


def _hgrn_sample(z, y_all, lbp, gain, s0, s_all, layer, row0, nb, nt):
    nseq = GLA_ROWS // nt
    rb0 = row0 // GLA_ROWS
    wb = HGRN_HB * LANES

    def zspec(k):
        c0 = (S5_WIDTH + k * HGRN_WIDTH) // wb
        return pl.BlockSpec((GLA_ROWS, wb), lambda h, i: (rb0 + i, c0 + h))

    par = pl.BlockSpec((lbp.shape[0], wb), lambda h, i: (0, h))
    sspec = pl.BlockSpec((None, nseq, HGRN_HB, HGRN_DK, HGRN_DV), lambda h, i: (layer, i, h, 0, 0))
    anyspec = pl.BlockSpec(memory_space=pl.ANY)
    carried = (y_all,) if s_all is None else (y_all, s_all)
    aliases = {7: 0} if s_all is None else {7: 0, 8: 1}
    return pl.pallas_call(
        functools.partial(_hgrn_sample_kernel, layer=layer, nt=nt),
        grid=(HGRN_HEADS // HGRN_HB, nb // nseq),
        in_specs=[zspec(0), zspec(1), zspec(2), zspec(3), par, par, sspec] + [anyspec] * len(carried),
        out_specs=[pl.BlockSpec((GLA_ROWS, wb), lambda h, i: (rb0 + i, h)), sspec],
        out_shape=[jax.ShapeDtypeStruct(y_all.shape, F32),
                   jax.ShapeDtypeStruct(s0.shape, s0.dtype)],
        input_output_aliases=aliases,
        compiler_params=_params("parallel", "parallel"),
        name="hgrn_sample",
    )(z, z, z, z, lbp, gain, s0, *carried)


def _mixout_kernel(h_ref, ys_ref, yh_ref, wglu_ref, bglu_ref, wo_ref, gpost_ref, o_ref):
    ys = ys_ref[...]
    gate = jax.nn.sigmoid(_dot(ys.astype(BF16), wglu_ref[...]) + bglu_ref[...])
    out = (_dot((ys * gate).astype(BF16), wo_ref[:S5_WIDTH, :])
           + _dot(yh_ref[...].astype(BF16), wo_ref[S5_WIDTH:, :]))
    o_ref[...] = h_ref[...] + _rms(out, gpost_ref[...])


def _mixout(h, ys, yh, w_glu, b_glu, w_out, gpost):
    m = h.shape[0]
    tm = _row_tile(m)
    full = lambda shape: pl.BlockSpec(shape, lambda i: (0, 0))
    return pl.pallas_call(
        _mixout_kernel,
        grid=(m // tm,),
        in_specs=[
            pl.BlockSpec((tm, D_MODEL), lambda i: (i, 0)),
            pl.BlockSpec((tm, S5_WIDTH), lambda i: (i, 0)),
            pl.BlockSpec((tm, HGRN_WIDTH), lambda i: (i, 0)),
            full((S5_WIDTH, S5_WIDTH)), full((1, S5_WIDTH)),
            full((D_MODEL, D_MODEL)), full((1, D_MODEL)),
        ],
        out_specs=pl.BlockSpec((tm, D_MODEL), lambda i: (i, 0)),
        out_shape=jax.ShapeDtypeStruct((m, D_MODEL), F32),
        compiler_params=_params("parallel"),
        name="mixout",
    )(h, ys, yh, w_glu, b_glu, w_out, gpost)


def kernel(x_prompt, x_sample, state_s5_re, state_s5_im, state_hgrn, norm_pre, norm_post, ffn1_w_gate, ffn1_w_up, ffn1_w_down, ffn2_w_gate, ffn2_w_up, ffn2_w_down, w_in, w_out, s5_lam_re, s5_lam_im, s5_log_dt, s5_b_re, s5_b_im, s5_c_re, s5_c_im, s5_d, s5_w_glu, s5_b_glu, hgrn_lb, hgrn_norm):
    batch, seq, _ = x_prompt.shape
    nb, nt, _ = x_sample.shape
    depth = norm_pre.shape[0]
    mp = batch * seq
    sdt = state_hgrn.dtype

    h = jnp.concatenate([x_prompt.reshape(mp, D_MODEL), x_sample.reshape(nb * nt, D_MODEL)], axis=0)

    are, aim, bb_re, bb_im = _s5_discretise(s5_lam_re, s5_lam_im, s5_log_dt, s5_b_re, s5_b_im)
    bw, cw = _s5_block_weights(bb_re, bb_im, s5_c_re, s5_c_im)

    vec = lambda a: a.reshape(1, -1)

    small = ("re_p", "im_p", "h_p", "re_s", "im_s")
    outs = {k: [] for k in small}
    h_s_all = None
    for l in range(depth):
        h = _ffn(h, vec(norm_pre[l, 0]), vec(norm_post[l, 0]),
                 ffn1_w_gate, ffn1_w_up, ffn1_w_down, l)

        z = _mixin(h, vec(norm_pre[l, 1]), w_in, l)
        d = vec(s5_d[l])
        ys, re_p, im_p = _s5_prompt(z, bw[l], cw[l], d, are[l], aim[l], batch, seq)
        ys, re_s, im_s = _s5_sample(z, ys, bw[l], cw[l], d, are[l], aim[l],
                                    state_s5_re[l], state_s5_im[l], mp, nb, nt)
        yh, h_p = _hgrn_prompt(z, hgrn_lb, hgrn_norm, l, batch, seq)
        yh, h_s_all = _hgrn_sample(z, yh, hgrn_lb, hgrn_norm, state_hgrn, h_s_all, l, mp, nb, nt)
        h = _mixout(h, ys, yh, s5_w_glu[l].astype(BF16), vec(s5_b_glu[l]), w_out[l].astype(BF16),
                    vec(norm_post[l, 1]))

        h = _ffn(h, vec(norm_pre[l, 2]), vec(norm_post[l, 2]),
                 ffn2_w_gate, ffn2_w_up, ffn2_w_down, l)
        for k, a in zip(small, (re_p, im_p, h_p, re_s, im_s)):
            outs[k].append(a)

    stack = lambda k: jnp.stack(outs[k]).astype(sdt)
    return (h[:mp].reshape(batch, seq, D_MODEL), h[mp:].reshape(nb, nt, D_MODEL),
            stack("re_p"), stack("im_p"), stack("h_p"),
            stack("re_s"), stack("im_s"), h_s_all)
```

```python
import functools

import jax
import jax.numpy as jnp
from jax import lax
from jax.experimental import pallas as pl
from jax.experimental.pallas import tpu as pltpu

F32 = jnp.float32
BF16 = jnp.bfloat16

D_MODEL = 2048
S5_WIDTH = 1024
S5_GROUP = 16
S5_GROUPS = 64
S5_STATE = 64
HGRN_WIDTH = 1024
HGRN_DK = 128
HGRN_DV = 128
HGRN_HEADS = 8
D_FF = 5504
IN_WIDTH = S5_WIDTH + 4 * HGRN_WIDTH
EPS = 1e-6

LANES = 128
SUBLANES = 8
VMEM_LIMIT_BYTES = 60 * 1024 * 1024

TOKEN_TILE = 1024
FF_TILE = 512
MIXIN_ROWS = 1536
IN_TILE = 512
NORM_ROWS = 256
HGRN_HB = 8
S5_GB = 8
S5_NBLK = S5_GROUPS // S5_GB
S5_STEP_BLOCKS = 2
S5_SW = S5_GB * S5_STATE
S5_CW = S5_GB * S5_GROUP
GLA_ROWS = 64
GLA_DIAG = 7


def _params(*sem):
    return pltpu.CompilerParams(dimension_semantics=sem, vmem_limit_bytes=VMEM_LIMIT_BYTES)


def _row_tile(m, largest=512):
    for t in (1024, 512, 256, 128, 64):
        if t <= largest and m % t == 0:
            return t
    raise ValueError(f"token count {m} must be a multiple of 64")


def _rms(x, gain):
    return x * lax.rsqrt(jnp.mean(x * x, axis=-1, keepdims=True) + EPS) * gain


def _dot(a, b):
    return jnp.dot(a, b, preferred_element_type=F32)


def _for_row_chunks(nrows, fn, rows_per_slice=NORM_ROWS):
    step = min(rows_per_slice, nrows)
    for c in range(nrows // step):
        fn(pl.ds(c * step, step))


def _ffn_kernel(x_ref, gpre_ref, gpost_ref, wg_ref, wu_ref, wd_ref, o_ref, xn_ref, *, nj):
    j = pl.program_id(1)
    nrows = x_ref.shape[0]
    nvalid_last = D_FF - (nj - 1) * FF_TILE

    def weights(last):
        wd = wd_ref[...]
        if last:
            row = lax.broadcasted_iota(jnp.int32, wd.shape, 0)
            wd = jnp.where(row < nvalid_last, wd, 0.0)
        return wg_ref[...].astype(BF16), wu_ref[...].astype(BF16), wd.astype(BF16)

    def swiglu(r, w, last):
        xn = xn_ref[r, :]
        a = jax.nn.silu(_dot(xn, w[0])) * _dot(xn, w[1])
        if last:
            col = lax.broadcasted_iota(jnp.int32, a.shape, 1)
            a = jnp.where(col < nvalid_last, a, 0.0)
        return _dot(a.astype(BF16), w[2])

    @pl.when(j == 0)
    def _():
        w = weights(False)

        def first(r):
            xn_ref[r, :] = _rms(x_ref[r, :], gpre_ref[...]).astype(BF16)
            o_ref[r, :] = swiglu(r, w, False)

        _for_row_chunks(nrows, first)

    @pl.when((j > 0) & (j < nj - 1))
    def _():
        w = weights(False)

        def middle(r):
            o_ref[r, :] += swiglu(r, w, False)

        _for_row_chunks(nrows, middle, 2 * NORM_ROWS)

    @pl.when(j == nj - 1)
    def _():
        w = weights(True)

        def last(r):
            acc = o_ref[r, :] + swiglu(r, w, True)
            o_ref[r, :] = x_ref[r, :] + 0.5 * _rms(acc, gpost_ref[...])

        _for_row_chunks(nrows, last)


def _ffn(h, gpre, gpost, wg, wu, wd, layer):
    m = h.shape[0]
    tm = _row_tile(m, TOKEN_TILE)
    nj = pl.cdiv(D_FF, FF_TILE)
    assert nj >= 2
    return pl.pallas_call(
        functools.partial(_ffn_kernel, nj=nj),
        grid=(m // tm, nj),
        in_specs=[
            pl.BlockSpec((tm, D_MODEL), lambda i, j: (i, 0), pipeline_mode=pl.Buffered(1)),
            pl.BlockSpec((1, D_MODEL), lambda i, j: (0, 0)),
            pl.BlockSpec((1, D_MODEL), lambda i, j: (0, 0)),
            pl.BlockSpec((None, D_MODEL, FF_TILE), lambda i, j: (layer, 0, j)),
            pl.BlockSpec((None, D_MODEL, FF_TILE), lambda i, j: (layer, 0, j)),
            pl.BlockSpec((None, FF_TILE, D_MODEL), lambda i, j: (layer, j, 0)),
        ],
        out_specs=pl.BlockSpec((tm, D_MODEL), lambda i, j: (i, 0)),
        out_shape=jax.ShapeDtypeStruct((m, D_MODEL), F32),
        scratch_shapes=[pltpu.VMEM((tm, D_MODEL), BF16)],
        compiler_params=_params("parallel", "arbitrary"),
        name="ffn",
    )(h, gpre, gpost, wg, wu, wd)


def _mixin_kernel(x_ref, gpre_ref, w_ref, o_ref, xn_ref):
    @pl.when(pl.program_id(1) == 0)
    def _():
        def norm_in(r):
            xn_ref[r, :] = _rms(x_ref[r, :], gpre_ref[...]).astype(BF16)

        _for_row_chunks(x_ref.shape[0], norm_in)

    o_ref[...] = _dot(xn_ref[...], w_ref[...].astype(BF16))


def _mixin(h, gpre, w_in, layer):
    m = h.shape[0]
    tm = MIXIN_ROWS if m % MIXIN_ROWS == 0 else _row_tile(m, TOKEN_TILE)
    return pl.pallas_call(
        _mixin_kernel,
        grid=(m // tm, IN_WIDTH // IN_TILE),
        in_specs=[
            pl.BlockSpec((tm, D_MODEL), lambda i, j: (i, 0)),
            pl.BlockSpec((1, D_MODEL), lambda i, j: (0, 0)),
            pl.BlockSpec((None, D_MODEL, IN_TILE), lambda i, j: (layer, 0, j)),
        ],
        out_specs=pl.BlockSpec((tm, IN_TILE), lambda i, j: (i, j)),
        out_shape=jax.ShapeDtypeStruct((m, IN_WIDTH), F32),
        scratch_shapes=[pltpu.VMEM((tm, D_MODEL), BF16)],
        compiler_params=_params("parallel", "arbitrary"),
        name="mixin",
    )(h, gpre, w_in)


def _s5_disc_kernel(lre_ref, lim_ref, ldt_ref, bre_ref, bim_ref,
                    are_ref, aim_ref, bbre_ref, bbim_ref):
    lam_re = lre_ref[...]
    lam_im = lim_ref[...]
    dt = jnp.exp(ldt_ref[...])
    mag = jnp.exp(lam_re * dt)
    ang = lam_im * dt
    abar_re = mag * jnp.cos(ang)
    abar_im = mag * jnp.sin(ang)
    p = abar_re - 1.0
    den = lam_re * lam_re + lam_im * lam_im
    z_re = (p * lam_re + abar_im * lam_im) / den
    z_im = (abar_im * lam_re - p * lam_im) / den
    are_ref[...] = abar_re
    aim_ref[...] = abar_im
    b_re = bre_ref[...]
    b_im = bim_ref[...]
    bbre_ref[...] = z_re * b_re - z_im * b_im
    bbim_ref[...] = z_re * b_im + z_im * b_re


def _s5_discretise(lam_re, lam_im, log_dt, b_re, b_im):
    depth = lam_re.shape[0]
    gn = S5_GROUPS * S5_STATE
    flat = lambda a: a.reshape(depth, 1, gn)
    ldt = jnp.broadcast_to(log_dt[:, :, None], lam_re.shape)
    chan_major = lambda b: jnp.transpose(b, (0, 3, 1, 2)).reshape(depth, S5_GROUP, gn)
    row = pl.BlockSpec((None, 1, gn), lambda l: (l, 0, 0))
    mat = pl.BlockSpec((None, S5_GROUP, gn), lambda l: (l, 0, 0))
    return pl.pallas_call(
        _s5_disc_kernel,
        grid=(depth,),
        in_specs=[row, row, row, mat, mat],
        out_specs=[row, row, mat, mat],
        out_shape=[jax.ShapeDtypeStruct((depth, 1, gn), F32)] * 2
        + [jax.ShapeDtypeStruct((depth, S5_GROUP, gn), F32)] * 2,
        compiler_params=_params("parallel"),
        name="s5_discretise",
    )(flat(lam_re), flat(lam_im), flat(ldt), chan_major(b_re), chan_major(b_im))


def _s5_block_weights(bb_re, bb_im, c_re, c_im):
    depth = bb_re.shape[0]
    eye = jnp.eye(S5_GB, dtype=F32)

    def in_map(bb):
        bb = bb.reshape(depth, S5_GROUP, S5_NBLK, S5_GB, S5_STATE)
        w = jnp.einsum("lcbgn,gh->lbgchn", bb, eye)
        return w.reshape(depth, S5_NBLK, S5_CW, S5_SW)

    def out_map(c):
        c = c.reshape(depth, S5_NBLK, S5_GB, S5_GROUP, S5_STATE)
        w = jnp.einsum("lbgcn,gh->lbgnhc", c, eye)
        return w.reshape(depth, S5_NBLK, S5_SW, S5_CW)

    bw = jnp.concatenate([in_map(bb_re), in_map(bb_im)], axis=-1).astype(BF16)
    cw = jnp.concatenate([out_map(c_re), -out_map(c_im)], axis=-2).astype(BF16)
    return bw, cw


def _s5_kernel(*refs, nseg, seglen, chained, ntc):
    if chained:
        (u_ref, bw_ref, cw_ref, d_ref, are_ref, aim_ref,
         y_ref, sre_ref, sim_ref, us_ref, up_ref, x_ref, yp_ref, carry_ref) = refs
    else:
        (u_ref, bw_ref, cw_ref, d_ref, are_ref, aim_ref, h0re_ref, h0im_ref, y_all_ref,
         y_ref, sre_ref, sim_ref, us_ref, up_ref, x_ref, yp_ref) = refs
    sw, cwid = S5_SW, S5_CW
    blocks = range(S5_STEP_BLOCKS)

    for k in blocks:
        us_ref[k] = u_ref[:, k * cwid:(k + 1) * cwid]
    for k in blocks:
        for i in range(seglen):
            up_ref[k, i * nseg:(i + 1) * nseg, :] = us_ref.at[k][pl.ds(i, nseg, stride=seglen), :]
        x_ref[k] = _dot(up_ref[k].astype(BF16), bw_ref[k])

    abar = []
    for k in blocks:
        ar = are_ref[:, k * sw:(k + 1) * sw]
        ai = aim_ref[:, k * sw:(k + 1) * sw]
        abar.append((ar, ai, jnp.broadcast_to(ar, (SUBLANES, sw)), jnp.broadcast_to(ai, (SUBLANES, sw))))

    def step(k, xr, xi, r0):
        _, _, arb, aib = abar[k]
        br = x_ref[k, pl.ds(r0, SUBLANES), :sw]
        bi = x_ref[k, pl.ds(r0, SUBLANES), sw:]
        return arb * xr - aib * xi + br, arb * xi + aib * xr + bi

    def step_store(k, xr, xi, r0):
        xr, xi = step(k, xr, xi, r0)
        x_ref[k, pl.ds(r0, SUBLANES), :sw] = xr
        x_ref[k, pl.ds(r0, SUBLANES), sw:] = xi
        return xr, xi

    if chained:
        tc = pl.program_id(2)

        @pl.when(tc == 0)
        def _():
            carry_ref[...] = jnp.zeros_like(carry_ref)

        for k in blocks:
            ar, ai, _, _ = abar[k]
            er = ei = jnp.zeros((SUBLANES, sw), F32)
            for i in range(seglen):
                er, ei = step(k, er, ei, i * SUBLANES)
            pr, pi = ar, ai
            for _ in range(seglen.bit_length() - 1):
                pr, pi = pr * pr - pi * pi, 2.0 * pr * pi
            cr = carry_ref[k, 0:1, :sw]
            ci = carry_ref[k, 0:1, sw:]
            starts_r, starts_i = [], []
            for j in range(SUBLANES):
                starts_r.append(cr)
                starts_i.append(ci)
                cr, ci = (pr * cr - pi * ci + er[j:j + 1, :],
                          pr * ci + pi * cr + ei[j:j + 1, :])
            carry_ref[k, 0:1, :sw] = cr
            carry_ref[k, 0:1, sw:] = ci
            xr = jnp.concatenate(starts_r, axis=0)
            xi = jnp.concatenate(starts_i, axis=0)
            for i in range(seglen):
                xr, xi = step_store(k, xr, xi, i * SUBLANES)

        @pl.when(tc == ntc - 1)
        def _():
            for k in blocks:
                sre_ref[:, k * sw:(k + 1) * sw] = carry_ref[k, 0:1, :sw]
                sim_ref[:, k * sw:(k + 1) * sw] = carry_ref[k, 0:1, sw:]
    else:
        def scan_group(sg, _):
            g0 = pl.multiple_of(sg * SUBLANES, SUBLANES)
            for k in blocks:
                xr = h0re_ref[pl.ds(g0, SUBLANES), k * sw:(k + 1) * sw]
                xi = h0im_ref[pl.ds(g0, SUBLANES), k * sw:(k + 1) * sw]
                for i in range(seglen):
                    xr, xi = step_store(k, xr, xi, pl.multiple_of(i * nseg + g0, SUBLANES))
                sre_ref[pl.ds(g0, SUBLANES), k * sw:(k + 1) * sw] = xr
                sim_ref[pl.ds(g0, SUBLANES), k * sw:(k + 1) * sw] = xi
            return 0

        lax.fori_loop(0, nseg // SUBLANES, scan_group, 0)

    for k in blocks:
        y = (_dot(x_ref[k].astype(BF16), cw_ref[k])
             + d_ref[:, k * cwid:(k + 1) * cwid] * up_ref[k])
        yp_ref[k] = jax.nn.gelu(y)
        for i in range(seglen):
            us_ref.at[k][pl.ds(i, nseg, stride=seglen), :] = yp_ref[k, i * nseg:(i + 1) * nseg, :]
        y_ref[:, k * cwid:(k + 1) * cwid] = us_ref[k]


def _s5_prompt(z, bw, cw, d, are, aim, batch, seq):
    nseg, seglen = SUBLANES, 64
    rows = nseg * seglen
    ntc = seq // rows
    gn = S5_GROUPS * S5_STATE
    nsb = S5_STEP_BLOCKS
    wspec = lambda shape: pl.BlockSpec((nsb,) + shape, lambda b, g, t: (g, 0, 0))
    vec = lambda w: pl.BlockSpec((1, nsb * w), lambda b, g, t: (0, g))
    st = pl.BlockSpec((None, 1, nsb * S5_SW), lambda b, g, t: (b, 0, g))
    slab = pltpu.VMEM((nsb, rows, S5_CW), F32)
    y, sre, sim = pl.pallas_call(
        functools.partial(_s5_kernel, nseg=nseg, seglen=seglen, chained=True, ntc=ntc),
        grid=(batch, S5_NBLK // nsb, ntc),
        in_specs=[
            pl.BlockSpec((rows, nsb * S5_CW), lambda b, g, t: (b * ntc + t, g)),
            wspec((S5_CW, 2 * S5_SW)),
            wspec((2 * S5_SW, S5_CW)),
            vec(S5_CW), vec(S5_SW), vec(S5_SW),
        ],
        out_specs=[pl.BlockSpec((rows, nsb * S5_CW), lambda b, g, t: (b * ntc + t, g)), st, st],
        out_shape=[jax.ShapeDtypeStruct((z.shape[0], S5_WIDTH), F32),
                   jax.ShapeDtypeStruct((batch, 1, gn), F32),
                   jax.ShapeDtypeStruct((batch, 1, gn), F32)],
        scratch_shapes=[slab, slab,
                        pltpu.VMEM((nsb, rows, 2 * S5_SW), F32),
                        slab,
                        pltpu.VMEM((nsb, SUBLANES, 2 * S5_SW), F32)],
        compiler_params=_params("parallel", "parallel", "arbitrary"),
        name="s5_prompt",
    )(z, bw, cw, d, are, aim)
    shape = (batch, S5_GROUPS, S5_STATE)
    return y, sre.reshape(shape), sim.reshape(shape)


def _s5_sample(z, y_all, bw, cw, d, are, aim, h0_re, h0_im, row0, nb, nt):
    rows = nb * nt
    gn = S5_GROUPS * S5_STATE
    rb = row0 // rows
    nsb = S5_STEP_BLOCKS
    wspec = lambda shape: pl.BlockSpec((nsb,) + shape, lambda g: (g, 0, 0))
    vec = lambda w: pl.BlockSpec((1, nsb * w), lambda g: (0, g))
    st = pl.BlockSpec((nb, nsb * S5_SW), lambda g: (0, g))
    slab = pltpu.VMEM((nsb, rows, S5_CW), F32)
    y, sre, sim = pl.pallas_call(
        functools.partial(_s5_kernel, nseg=nb, seglen=nt, chained=False, ntc=1),
        grid=(S5_NBLK // nsb,),
        in_specs=[
            pl.BlockSpec((rows, nsb * S5_CW), lambda g: (rb, g)),
            wspec((S5_CW, 2 * S5_SW)),
            wspec((2 * S5_SW, S5_CW)),
            vec(S5_CW), vec(S5_SW), vec(S5_SW), st, st,
            pl.BlockSpec(memory_space=pl.ANY),
        ],
        out_specs=[pl.BlockSpec((rows, nsb * S5_CW), lambda g: (rb, g)), st, st],
        out_shape=[jax.ShapeDtypeStruct(y_all.shape, F32),
                   jax.ShapeDtypeStruct((nb, gn), F32),
                   jax.ShapeDtypeStruct((nb, gn), F32)],
        input_output_aliases={8: 0},
        scratch_shapes=[slab, slab,
                        pltpu.VMEM((nsb, rows, 2 * S5_SW), F32),
                        slab],
        compiler_params=_params("parallel"),
        name="s5_sample",
    )(z, bw, cw, d, are, aim, h0_re.reshape(nb, gn), h0_im.reshape(nb, gn), y_all)
    shape = (nb, S5_GROUPS, S5_STATE)
    return y, sre.reshape(shape), sim.reshape(shape)


def _cumsum_rows(x, seq_len):
    ngroup = GLA_ROWS // SUBLANES
    y = x.reshape(ngroup, SUBLANES, LANES)
    r = lax.broadcasted_iota(jnp.int32, y.shape, 1)
    for d in (1, 2, 4):
        y = y + jnp.where(r >= d, pltpu.roll(y, d, axis=1), 0.0)
    if seq_len == SUBLANES:
        return y.reshape(GLA_ROWS, LANES)
    parts, acc = [], None
    for g in range(ngroup):
        if g % (seq_len // SUBLANES) == 0:
            acc = None
        part = y[g] if acc is None else y[g] + acc
        parts.append(part)
        acc = part[SUBLANES - 1:SUBLANES, :]
    return jnp.concatenate(parts, axis=0)


def _level_ref(b, level):
    half = 1 << level
    span = 2 * half
    if span >= SUBLANES:
        b3 = b.reshape(GLA_ROWS // span, span, LANES)
        return jnp.broadcast_to(b3[:, half - 1:half, :], b3.shape).reshape(GLA_ROWS, LANES)
    b3 = b.reshape(GLA_ROWS // SUBLANES, SUBLANES, LANES)
    r = lax.broadcasted_iota(jnp.int32, b3.shape, 1)
    nspan = SUBLANES // span
    ref = jnp.broadcast_to(b3[:, (nspan - 1) * span + half - 1:(nspan - 1) * span + half, :], b3.shape)
    for p in range(nspan - 2, -1, -1):
        piece = jnp.broadcast_to(b3[:, p * span + half - 1:p * span + half, :], b3.shape)
        ref = jnp.where(r < (p + 1) * span, piece, ref)
    return ref.reshape(GLA_ROWS, LANES)


def _gla_tables(seq_len):
    t = lax.broadcasted_iota(jnp.int32, (GLA_ROWS, GLA_ROWS), 0)
    s = lax.broadcasted_iota(jnp.int32, (GLA_ROWS, GLA_ROWS), 1)
    x = t ^ s
    level = jnp.zeros((GLA_ROWS, GLA_ROWS), jnp.int32)
    for k in range(1, 6):
        level = level + jnp.where(x >= (1 << k), 1, 0)
    level = jnp.where(s < t, level, jnp.where(s == t, GLA_DIAG, -1))
    rows = lax.broadcasted_iota(jnp.int32, (GLA_ROWS, LANES), 0)
    return level, rows


def _gla_block(q, fz, v, lb, states, tables, seq_len, transposed_state):
    level, rows = tables
    nseq = GLA_ROWS // seq_len
    nlev = seq_len.bit_length() - 1
    qc = jax.nn.silu(q)
    fg = lb + (1.0 - lb) * jax.nn.sigmoid(fz)
    kc = 1.0 - fg
    vb = v.astype(BF16)
    b = _cumsum_rows(jnp.log2(fg), seq_len)

    nt = (((1,), (1,)), ((), ()))
    att = jnp.where(level == GLA_DIAG,
                    lax.dot_general(qc.astype(BF16), kc.astype(BF16), nt, preferred_element_type=F32),
                    0.0)
    for lev in range(nlev):
        half = 1 << lev
        if half >= SUBLANES:
            shape3 = (GLA_ROWS // (2 * half), 2 * half, LANES)
            b3, q3, k3 = b.reshape(shape3), qc.reshape(shape3), kc.reshape(shape3)
            ref = b3[:, half - 1:half, :]
            zero = jnp.zeros((shape3[0], half, LANES), F32)
            qh = jnp.concatenate([zero, q3[:, half:, :] * jnp.exp2(b3[:, half:, :] - ref)], axis=1)
            kh = jnp.concatenate([k3[:, :half, :] * jnp.exp2(ref - b3[:, :half, :]), zero], axis=1)
            qh = qh.reshape(GLA_ROWS, LANES).astype(BF16)
            kh = kh.reshape(GLA_ROWS, LANES).astype(BF16)
        else:
            w = jnp.exp2(-jnp.abs(b - _level_ref(b, lev)))
            upper = ((rows >> lev) & 1) == 1
            qh = jnp.where(upper, qc * w, 0.0).astype(BF16)
            kh = jnp.where(upper, 0.0, kc * w).astype(BF16)
        att = jnp.where(level == lev,
                        lax.dot_general(qh, kh, nt, preferred_element_type=F32), att)
    o = _dot(att.astype(BF16), vb)

    b3 = b.reshape(nseq, seq_len, LANES)
    blast = jnp.broadcast_to(b3[:, seq_len - 1:seq_len, :], b3.shape).reshape(GLA_ROWS, LANES)
    qe = (qc * jnp.exp2(b)).astype(BF16)
    kd = (kc * jnp.exp2(blast - b)).astype(BF16)
    tn = (((0,), (0,)), ((), ()))
    o_inter, new_states = [], []
    for n in range(nseq):
        sl = slice(n * seq_len, (n + 1) * seq_len)
        s_n = states[n]
        e_row = jnp.exp2(blast[n * seq_len:n * seq_len + 1, :])
        if transposed_state:
            o_inter.append(lax.dot_general(qe[sl], s_n.astype(BF16), nt, preferred_element_type=F32))
            upd = lax.dot_general(vb[sl], kd[sl], tn, preferred_element_type=F32)
            new_states.append(e_row * s_n + upd)
        else:
            o_inter.append(_dot(qe[sl], s_n.astype(BF16)))
            upd = lax.dot_general(kd[sl], vb[sl], tn, preferred_element_type=F32)
            e_col = jnp.transpose(jnp.broadcast_to(e_row, (HGRN_DK, LANES)))
            new_states.append(e_col * s_n + upd)
    o = o + (o_inter[0] if nseq == 1 else jnp.concatenate(o_inter, axis=0))
    return o, new_states


def _lower_bound(lbp, layer):
    if layer == 0:
        return jnp.zeros((1, lbp.shape[1]), F32)
    e = jnp.exp(lbp - jnp.max(lbp, axis=0, keepdims=True))
    p = e / jnp.sum(e, axis=0, keepdims=True)
    return jnp.sum(p[1:layer + 1, :], axis=0, keepdims=True)


def _gla_finish(o, g, gain):
    o = o * lax.rsqrt(jnp.mean(o * o, axis=-1, keepdims=True) + EPS)
    return o * gain * jax.nn.silu(g)


def _hgrn_prompt_kernel(q_ref, f_ref, v_ref, g_ref, lbp_ref, gain_ref, o_ref, sout_ref, s_ref,
                        *, layer, nchunk, ntb):
    tb = pl.program_id(2)

    @pl.when(tb == 0)
    def _():
        s_ref[...] = jnp.zeros_like(s_ref)

    tables = _gla_tables(GLA_ROWS)
    lb = _lower_bound(lbp_ref[...], layer)
    gain = gain_ref[layer:layer + 1, :]

    def chunk(c, _):
        r = pl.ds(pl.multiple_of(c * GLA_ROWS, GLA_ROWS), GLA_ROWS)
        for hh in range(HGRN_HB):
            cs = slice(hh * LANES, (hh + 1) * LANES)
            o, (s_new,) = _gla_block(q_ref[r, cs], f_ref[r, cs], v_ref[r, cs], lb[:, cs],
                                     [s_ref[hh]], tables, GLA_ROWS, True)
            s_ref[hh] = s_new
            o_ref[r, cs] = _gla_finish(o, g_ref[r, cs], gain[:, cs])
        return 0

    lax.fori_loop(0, nchunk, chunk, 0)

    @pl.when(tb == ntb - 1)
    def _():
        for hh in range(HGRN_HB):
            sout_ref[hh] = jnp.transpose(s_ref[hh])


def _hgrn_prompt(z, lbp, gain, layer, batch, seq):
    tb_rows = 512
    ntb = seq // tb_rows
    wb = HGRN_HB * LANES
    nhb = HGRN_HEADS // HGRN_HB

    def zspec(k):
        c0 = (S5_WIDTH + k * HGRN_WIDTH) // wb
        return pl.BlockSpec((tb_rows, wb), lambda b, h, t: (b * ntb + t, c0 + h))

    par = pl.BlockSpec((lbp.shape[0], wb), lambda b, h, t: (0, h))
    return pl.pallas_call(
        functools.partial(_hgrn_prompt_kernel, layer=layer, nchunk=tb_rows // GLA_ROWS, ntb=ntb),
        grid=(batch, nhb, ntb),
        in_specs=[zspec(0), zspec(1), zspec(2), zspec(3), par, par],
        out_specs=[pl.BlockSpec((tb_rows, wb), lambda b, h, t: (b * ntb + t, h)),
                   pl.BlockSpec((None, HGRN_HB, HGRN_DK, HGRN_DV), lambda b, h, t: (b, h, 0, 0))],
        out_shape=[jax.ShapeDtypeStruct((z.shape[0], HGRN_WIDTH), F32),
                   jax.ShapeDtypeStruct((batch, HGRN_HEADS, HGRN_DK, HGRN_DV), F32)],
        scratch_shapes=[pltpu.VMEM((HGRN_HB, HGRN_DK, HGRN_DV), F32)],
        compiler_params=_params("parallel", "parallel", "arbitrary"),
        name="hgrn_prompt",
    )(z, z, z, z, lbp, gain)


def _hgrn_sample_kernel(q_ref, f_ref, v_ref, g_ref, lbp_ref, gain_ref, s0_ref, *rest, layer, nt):
    o_ref, sout_ref = rest[-2:]
    nseq = GLA_ROWS // nt
    tables = _gla_tables(nt)
    lb = _lower_bound(lbp_ref[...], layer)
    gain = gain_ref[layer:layer + 1, :]
    for hh in range(HGRN_HB):
        cs = slice(hh * LANES, (hh + 1) * LANES)
        o, new_states = _gla_block(q_ref[:, cs], f_ref[:, cs], v_ref[:, cs], lb[:, cs],
                                   [s0_ref[n, hh] for n in range(nseq)], tables, nt, False)
        for n in range(nseq):
            sout_ref[n, hh] = new_states[n].astype(sout_ref.dtype)
        o_ref[:, cs] = _gla_finish(o, g_ref[:, cs], gain[:, cs])


def _hgrn_sample(z, y_all, lbp, gain, s0, s_all, layer, row0, nb, nt):
    nseq = GLA_ROWS // nt
    rb0 = row0 // GLA_ROWS
    wb = HGRN_HB * LANES

    def zspec(k):
        c0 = (S5_WIDTH + k * HGRN_WIDTH) // wb
        return pl.BlockSpec((GLA_ROWS, wb), lambda h, i: (rb0 + i, c0 + h))

    par = pl.BlockSpec((lbp.shape[0], wb), lambda h, i: (0, h))
    sspec = pl.BlockSpec((None, nseq, HGRN_HB, HGRN_DK, HGRN_DV), lambda h, i: (layer, i, h, 0, 0))
    anyspec = pl.BlockSpec(memory_space=pl.ANY)
    carried = (y_all,) if s_all is None else (y_all, s_all)
    aliases = {7: 0} if s_all is None else {7: 0, 8: 1}
    return pl.pallas_call(
        functools.partial(_hgrn_sample_kernel, layer=layer, nt=nt),
        grid=(HGRN_HEADS // HGRN_HB, nb // nseq),
        in_specs=[zspec(0), zspec(1), zspec(2), zspec(3), par, par, sspec] + [anyspec] * len(carried),
        out_specs=[pl.BlockSpec((GLA_ROWS, wb), lambda h, i: (rb0 + i, h)), sspec],
        out_shape=[jax.ShapeDtypeStruct(y_all.shape, F32),
                   jax.ShapeDtypeStruct(s0.shape, s0.dtype)],
        input_output_aliases=aliases,
        compiler_params=_params("parallel", "parallel"),
        name="hgrn_sample",
    )(z, z, z, z, lbp, gain, s0, *carried)


def _mixout_kernel(h_ref, ys_ref, yh_ref, wglu_ref, bglu_ref, wo_ref, gpost_ref, o_ref):
    ys = ys_ref[...]
    gate = jax.nn.sigmoid(_dot(ys.astype(BF16), wglu_ref[...]) + bglu_ref[...])
    out = (_dot((ys * gate).astype(BF16), wo_ref[:S5_WIDTH, :])
           + _dot(yh_ref[...].astype(BF16), wo_ref[S5_WIDTH:, :]))
    o_ref[...] = h_ref[...] + _rms(out, gpost_ref[...])


def _mixout(h, ys, yh, w_glu, b_glu, w_out, gpost):
    m = h.shape[0]
    tm = _row_tile(m)
    full = lambda shape: pl.BlockSpec(shape, lambda i: (0, 0))
    return pl.pallas_call(
        _mixout_kernel,
        grid=(m // tm,),
        in_specs=[
            pl.BlockSpec((tm, D_MODEL), lambda i: (i, 0)),
            pl.BlockSpec((tm, S5_WIDTH), lambda i: (i, 0)),
            pl.BlockSpec((tm, HGRN_WIDTH), lambda i: (i, 0)),
            full((S5_WIDTH, S5_WIDTH)), full((1, S5_WIDTH)),
            full((D_MODEL, D_MODEL)), full((1, D_MODEL)),
        ],
        out_specs=pl.BlockSpec((tm, D_MODEL), lambda i: (i, 0)),
        out_shape=jax.ShapeDtypeStruct((m, D_MODEL), F32),
        compiler_params=_params("parallel"),
        name="mixout",
    )(h, ys, yh, w_glu, b_glu, w_out, gpost)


def kernel(x_prompt, x_sample, state_s5_re, state_s5_im, state_hgrn, norm_pre, norm_post, ffn1_w_gate, ffn1_w_up, ffn1_w_down, ffn2_w_gate, ffn2_w_up, ffn2_w_down, w_in, w_out, s5_lam_re, s5_lam_im, s5_log_dt, s5_b_re, s5_b_im, s5_c_re, s5_c_im, s5_d, s5_w_glu, s5_b_glu, hgrn_lb, hgrn_norm):
    batch, seq, _ = x_prompt.shape
    nb, nt, _ = x_sample.shape
    depth = norm_pre.shape[0]
    mp = batch * seq
    sdt = state_hgrn.dtype

    h = jnp.concatenate([x_prompt.reshape(mp, D_MODEL), x_sample.reshape(nb * nt, D_MODEL)], axis=0)

    are, aim, bb_re, bb_im = _s5_discretise(s5_lam_re, s5_lam_im, s5_log_dt, s5_b_re, s5_b_im)
    bw, cw = _s5_block_weights(bb_re, bb_im, s5_c_re, s5_c_im)

    vec = lambda a: a.reshape(1, -1)

    small = ("re_p", "im_p", "h_p", "re_s", "im_s")
    outs = {k: [] for k in small}
    h_s_all = None
    for l in range(depth):
        h = _ffn(h, vec(norm_pre[l, 0]), vec(norm_post[l, 0]),
                 ffn1_w_gate, ffn1_w_up, ffn1_w_down, l)

        z = _mixin(h, vec(norm_pre[l, 1]), w_in, l)
        d = vec(s5_d[l])
        ys, re_p, im_p = _s5_prompt(z, bw[l], cw[l], d, are[l], aim[l], batch, seq)
        ys, re_s, im_s = _s5_sample(z, ys, bw[l], cw[l], d, are[l], aim[l],
                                    state_s5_re[l], state_s5_im[l], mp, nb, nt)
        yh, h_p = _hgrn_prompt(z, hgrn_lb, hgrn_norm, l, batch, seq)
        yh, h_s_all = _hgrn_sample(z, yh, hgrn_lb, hgrn_norm, state_hgrn, h_s_all, l, mp, nb, nt)
        h = _mixout(h, ys, yh, s5_w_glu[l].astype(BF16), vec(s5_b_glu[l]), w_out[l].astype(BF16),
                    vec(norm_post[l, 1]))

        h = _ffn(h, vec(norm_pre[l, 2]), vec(norm_post[l, 2]),
                 ffn2_w_gate, ffn2_w_up, ffn2_w_down, l)
        for k, a in zip(small, (re_p, im_p, h_p, re_s, im_s)):
            outs[k].append(a)

    stack = lambda k: jnp.stack(outs[k]).astype(sdt)
    return (h[:mp].reshape(batch, seq, D_MODEL), h[mp:].reshape(nb, nt, D_MODEL),
            stack("re_p"), stack("im_p"), stack("h_p"),
            stack("re_s"), stack("im_s"), h_s_all)
```

```python
import functools

import jax
import jax.numpy as jnp
from jax import lax
from jax.experimental import pallas as pl
from jax.experimental.pallas import tpu as pltpu

F32 = jnp.float32
BF16 = jnp.bfloat16

D_MODEL = 2048
S5_WIDTH = 1024
S5_GROUP = 16
S5_GROUPS = 64
S5_STATE = 64
HGRN_WIDTH = 1024
HGRN_DK = 128
HGRN_DV = 128
HGRN_HEADS = 8
D_FF = 5504
IN_WIDTH = S5_WIDTH + 4 * HGRN_WIDTH
EPS = 1e-6

LANES = 128
SUBLANES = 8
VMEM_LIMIT_BYTES = 60 * 1024 * 1024

TOKEN_TILE = 1024
FF_TILE = 256
MIXIN_ROWS = 1536
IN_TILE = 512
NORM_ROWS = 256
HGRN_HB = 8
S5_GB = 8
S5_NBLK = S5_GROUPS // S5_GB
S5_STEP_BLOCKS = 4
S5_SW = S5_GB * S5_STATE
S5_CW = S5_GB * S5_GROUP
GLA_ROWS = 64
GLA_DIAG = 7


def _params(*sem):
    return pltpu.CompilerParams(dimension_semantics=sem, vmem_limit_bytes=VMEM_LIMIT_BYTES)


def _row_tile(m, largest=512):
    for t in (1024, 512, 256, 128, 64):
        if t <= largest and m % t == 0:
            return t
    raise ValueError(f"token count {m} must be a multiple of 64")


def _rms(x, gain):
    return x * lax.rsqrt(jnp.mean(x * x, axis=-1, keepdims=True) + EPS) * gain


def _dot(a, b):
    return jnp.dot(a, b, preferred_element_type=F32)


def _for_row_chunks(nrows, fn, rows_per_slice=NORM_ROWS):
    step = min(rows_per_slice, nrows)
    for c in range(nrows // step):
        fn(pl.ds(c * step, step))


def _ffn_kernel(x_ref, gpre_ref, gpost_ref, wg_ref, wu_ref, wd_ref, o_ref, xn_ref, *, nj):
    j = pl.program_id(1)
    nrows = x_ref.shape[0]
    nvalid_last = D_FF - (nj - 1) * FF_TILE

    def weights(last):
        wd = wd_ref[...]
        if last:
            row = lax.broadcasted_iota(jnp.int32, wd.shape, 0)
            wd = jnp.where(row < nvalid_last, wd, 0.0)
        return wg_ref[...].astype(BF16), wu_ref[...].astype(BF16), wd.astype(BF16)

    def swiglu(r, w, last):
        xn = xn_ref[r, :]
        a = jax.nn.silu(_dot(xn, w[0])) * _dot(xn, w[1])
        if last:
            col = lax.broadcasted_iota(jnp.int32, a.shape, 1)
            a = jnp.where(col < nvalid_last, a, 0.0)
        return _dot(a.astype(BF16), w[2])

    @pl.when(j == 0)
    def _():
        w = weights(False)

        def first(r):
            xn_ref[r, :] = _rms(x_ref[r, :], gpre_ref[...]).astype(BF16)
            o_ref[r, :] = swiglu(r, w, False)

        _for_row_chunks(nrows, first)

    @pl.when((j > 0) & (j < nj - 1))
    def _():
        w = weights(False)

        def middle(r):
            o_ref[r, :] += swiglu(r, w, False)

        _for_row_chunks(nrows, middle, nrows)

    @pl.when(j == nj - 1)
    def _():
        w = weights(True)

        def last(r):
            acc = o_ref[r, :] + swiglu(r, w, True)
            o_ref[r, :] = x_ref[r, :] + 0.5 * _rms(acc, gpost_ref[...])

        _for_row_chunks(nrows, last)


def _ffn(h, gpre, gpost, wg, wu, wd, layer):
    m = h.shape[0]
    tm = _row_tile(m, TOKEN_TILE)
    nj = pl.cdiv(D_FF, FF_TILE)
    assert nj >= 2
    return pl.pallas_call(
        functools.partial(_ffn_kernel, nj=nj),
        grid=(m // tm, nj),
        in_specs=[
            pl.BlockSpec((tm, D_MODEL), lambda i, j: (i, 0)),
            pl.BlockSpec((1, D_MODEL), lambda i, j: (0, 0)),
            pl.BlockSpec((1, D_MODEL), lambda i, j: (0, 0)),
            pl.BlockSpec((None, D_MODEL, FF_TILE), lambda i, j: (layer, 0, j)),
            pl.BlockSpec((None, D_MODEL, FF_TILE), lambda i, j: (layer, 0, j)),
            pl.BlockSpec((None, FF_TILE, D_MODEL), lambda i, j: (layer, j, 0)),
        ],
        out_specs=pl.BlockSpec((tm, D_MODEL), lambda i, j: (i, 0)),
        out_shape=jax.ShapeDtypeStruct((m, D_MODEL), F32),
        scratch_shapes=[pltpu.VMEM((tm, D_MODEL), BF16)],
        compiler_params=_params("parallel", "arbitrary"),
        name="ffn",
    )(h, gpre, gpost, wg, wu, wd)


def _mixin_kernel(x_ref, gpre_ref, w_ref, o_ref, xn_ref):
    @pl.when(pl.program_id(1) == 0)
    def _():
        def norm_in(r):
            xn_ref[r, :] = _rms(x_ref[r, :], gpre_ref[...]).astype(BF16)

        _for_row_chunks(x_ref.shape[0], norm_in)

    o_ref[...] = _dot(xn_ref[...], w_ref[...].astype(BF16))


def _mixin(h, gpre, w_in, layer):
    m = h.shape[0]
    tm = MIXIN_ROWS if m % MIXIN_ROWS == 0 else _row_tile(m, TOKEN_TILE)
    return pl.pallas_call(
        _mixin_kernel,
        grid=(m // tm, IN_WIDTH // IN_TILE),
        in_specs=[
            pl.BlockSpec((tm, D_MODEL), lambda i, j: (i, 0)),
            pl.BlockSpec((1, D_MODEL), lambda i, j: (0, 0)),
            pl.BlockSpec((None, D_MODEL, IN_TILE), lambda i, j: (layer, 0, j)),
        ],
        out_specs=pl.BlockSpec((tm, IN_TILE), lambda i, j: (i, j)),
        out_shape=jax.ShapeDtypeStruct((m, IN_WIDTH), F32),
        scratch_shapes=[pltpu.VMEM((tm, D_MODEL), BF16)],
        compiler_params=_params("parallel", "arbitrary"),
        name="mixin",
    )(h, gpre, w_in)


def _s5_disc_kernel(lre_ref, lim_ref, ldt_ref, bre_ref, bim_ref,
                    are_ref, aim_ref, bbre_ref, bbim_ref):
    lam_re = lre_ref[...]
    lam_im = lim_ref[...]
    dt = jnp.exp(ldt_ref[...])
    mag = jnp.exp(lam_re * dt)
    ang = lam_im * dt
    abar_re = mag * jnp.cos(ang)
    abar_im = mag * jnp.sin(ang)
    p = abar_re - 1.0
    den = lam_re * lam_re + lam_im * lam_im
    z_re = (p * lam_re + abar_im * lam_im) / den
    z_im = (abar_im * lam_re - p * lam_im) / den
    are_ref[...] = abar_re
    aim_ref[...] = abar_im
    b_re = bre_ref[...]
    b_im = bim_ref[...]
    bbre_ref[...] = z_re * b_re - z_im * b_im
    bbim_ref[...] = z_re * b_im + z_im * b_re


def _s5_discretise(lam_re, lam_im, log_dt, b_re, b_im):
    depth = lam_re.shape[0]
    gn = S5_GROUPS * S5_STATE
    flat = lambda a: a.reshape(depth, 1, gn)
    ldt = jnp.broadcast_to(log_dt[:, :, None], lam_re.shape)
    chan_major = lambda b: jnp.transpose(b, (0, 3, 1, 2)).reshape(depth, S5_GROUP, gn)
    row = pl.BlockSpec((None, 1, gn), lambda l: (l, 0, 0))
    mat = pl.BlockSpec((None, S5_GROUP, gn), lambda l: (l, 0, 0))
    return pl.pallas_call(
        _s5_disc_kernel,
        grid=(depth,),
        in_specs=[row, row, row, mat, mat],
        out_specs=[row, row, mat, mat],
        out_shape=[jax.ShapeDtypeStruct((depth, 1, gn), F32)] * 2
        + [jax.ShapeDtypeStruct((depth, S5_GROUP, gn), F32)] * 2,
        compiler_params=_params("parallel"),
        name="s5_discretise",
    )(flat(lam_re), flat(lam_im), flat(ldt), chan_major(b_re), chan_major(b_im))


def _s5_block_weights(bb_re, bb_im, c_re, c_im):
    depth = bb_re.shape[0]
    eye = jnp.eye(S5_GB, dtype=F32)

    def in_map(bb):
        bb = bb.reshape(depth, S5_GROUP, S5_NBLK, S5_GB, S5_STATE)
        w = jnp.einsum("lcbgn,gh->lbgchn", bb, eye)
        return w.reshape(depth, S5_NBLK, S5_CW, S5_SW)

    def out_map(c):
        c = c.reshape(depth, S5_NBLK, S5_GB, S5_GROUP, S5_STATE)
        w = jnp.einsum("lbgcn,gh->lbgnhc", c, eye)
        return w.reshape(depth, S5_NBLK, S5_SW, S5_CW)

    bw = jnp.concatenate([in_map(bb_re), in_map(bb_im)], axis=-1).astype(BF16)
    cw = jnp.concatenate([out_map(c_re), -out_map(c_im)], axis=-2).astype(BF16)
    return bw, cw


def _s5_kernel(*refs, nseg, seglen, chained, ntc):
    if chained:
        (u_ref, bw_ref, cw_ref, d_ref, are_ref, aim_ref,
         y_ref, sre_ref, sim_ref, us_ref, up_ref, x_ref, yp_ref, carry_ref) = refs
    else:
        (u_ref, bw_ref, cw_ref, d_ref, are_ref, aim_ref, h0re_ref, h0im_ref, y_all_ref,
         y_ref, sre_ref, sim_ref, us_ref, up_ref, x_ref, yp_ref) = refs
    sw, cwid = S5_SW, S5_CW
    blocks = range(S5_STEP_BLOCKS)

    for k in blocks:
        us_ref[k] = u_ref[:, k * cwid:(k + 1) * cwid]
    for k in blocks:
        for i in range(seglen):
            up_ref[k, i * nseg:(i + 1) * nseg, :] = us_ref.at[k][pl.ds(i, nseg, stride=seglen), :]
        x_ref[k] = _dot(up_ref[k].astype(BF16), bw_ref[k])

    abar = []
    for k in blocks:
        ar = are_ref[:, k * sw:(k + 1) * sw]
        ai = aim_ref[:, k * sw:(k + 1) * sw]
        abar.append((ar, ai, jnp.broadcast_to(ar, (SUBLANES, sw)), jnp.broadcast_to(ai, (SUBLANES, sw))))

    def step(k, xr, xi, r0):
        _, _, arb, aib = abar[k]
        br = x_ref[k, pl.ds(r0, SUBLANES), :sw]
        bi = x_ref[k, pl.ds(r0, SUBLANES), sw:]
        return arb * xr - aib * xi + br, arb * xi + aib * xr + bi

    def step_store(k, xr, xi, r0):
        xr, xi = step(k, xr, xi, r0)
        x_ref[k, pl.ds(r0, SUBLANES), :sw] = xr
        x_ref[k, pl.ds(r0, SUBLANES), sw:] = xi
        return xr, xi

    if chained:
        tc = pl.program_id(2)

        @pl.when(tc == 0)
        def _():
            carry_ref[...] = jnp.zeros_like(carry_ref)

        for k in blocks:
            ar, ai, _, _ = abar[k]
            er = ei = jnp.zeros((SUBLANES, sw), F32)
            for i in range(seglen):
                er, ei = step(k, er, ei, i * SUBLANES)
            pr, pi = ar, ai
            for _ in range(seglen.bit_length() - 1):
                pr, pi = pr * pr - pi * pi, 2.0 * pr * pi
            cr = carry_ref[k, 0:1, :sw]
            ci = carry_ref[k, 0:1, sw:]
            starts_r, starts_i = [], []
            for j in range(SUBLANES):
                starts_r.append(cr)
                starts_i.append(ci)
                cr, ci = (pr * cr - pi * ci + er[j:j + 1, :],
                          pr * ci + pi * cr + ei[j:j + 1, :])
            carry_ref[k, 0:1, :sw] = cr
            carry_ref[k, 0:1, sw:] = ci
            xr = jnp.concatenate(starts_r, axis=0)
            xi = jnp.concatenate(starts_i, axis=0)
            for i in range(seglen):
                xr, xi = step_store(k, xr, xi, i * SUBLANES)

        @pl.when(tc == ntc - 1)
        def _():
            for k in blocks:
                sre_ref[:, k * sw:(k + 1) * sw] = carry_ref[k, 0:1, :sw]
                sim_ref[:, k * sw:(k + 1) * sw] = carry_ref[k, 0:1, sw:]
    else:
        def scan_group(sg, _):
            g0 = pl.multiple_of(sg * SUBLANES, SUBLANES)
            for k in blocks:
                xr = h0re_ref[pl.ds(g0, SUBLANES), k * sw:(k + 1) * sw]
                xi = h0im_ref[pl.ds(g0, SUBLANES), k * sw:(k + 1) * sw]
                for i in range(seglen):
                    xr, xi = step_store(k, xr, xi, pl.multiple_of(i * nseg + g0, SUBLANES))
                sre_ref[pl.ds(g0, SUBLANES), k * sw:(k + 1) * sw] = xr
                sim_ref[pl.ds(g0, SUBLANES), k * sw:(k + 1) * sw] = xi
            return 0

        lax.fori_loop(0, nseg // SUBLANES, scan_group, 0)

    for k in blocks:
        y = (_dot(x_ref[k].astype(BF16), cw_ref[k])
             + d_ref[:, k * cwid:(k + 1) * cwid] * up_ref[k])
        yp_ref[k] = jax.nn.gelu(y)
        for i in range(seglen):
            us_ref.at[k][pl.ds(i, nseg, stride=seglen), :] = yp_ref[k, i * nseg:(i + 1) * nseg, :]
        y_ref[:, k * cwid:(k + 1) * cwid] = us_ref[k]


def _s5_prompt(z, bw, cw, d, are, aim, batch, seq):
    nseg, seglen = SUBLANES, 64
    rows = nseg * seglen
    ntc = seq // rows
    gn = S5_GROUPS * S5_STATE
    nsb = S5_STEP_BLOCKS
    wspec = lambda shape: pl.BlockSpec((nsb,) + shape, lambda b, g, t: (g, 0, 0))
    vec = lambda w: pl.BlockSpec((1, nsb * w), lambda b, g, t: (0, g))
    st = pl.BlockSpec((None, 1, nsb * S5_SW), lambda b, g, t: (b, 0, g))
    slab = pltpu.VMEM((nsb, rows, S5_CW), F32)
    y, sre, sim = pl.pallas_call(
        functools.partial(_s5_kernel, nseg=nseg, seglen=seglen, chained=True, ntc=ntc),
        grid=(batch, S5_NBLK // nsb, ntc),
        in_specs=[
            pl.BlockSpec((rows, nsb * S5_CW), lambda b, g, t: (b * ntc + t, g)),
            wspec((S5_CW, 2 * S5_SW)),
            wspec((2 * S5_SW, S5_CW)),
            vec(S5_CW), vec(S5_SW), vec(S5_SW),
        ],
        out_specs=[pl.BlockSpec((rows, nsb * S5_CW), lambda b, g, t: (b * ntc + t, g)), st, st],
        out_shape=[jax.ShapeDtypeStruct((z.shape[0], S5_WIDTH), F32),
                   jax.ShapeDtypeStruct((batch, 1, gn), F32),
                   jax.ShapeDtypeStruct((batch, 1, gn), F32)],
        scratch_shapes=[slab, slab,
                        pltpu.VMEM((nsb, rows, 2 * S5_SW), F32),
                        slab,
                        pltpu.VMEM((nsb, SUBLANES, 2 * S5_SW), F32)],
        compiler_params=_params("parallel", "parallel", "arbitrary"),
        name="s5_prompt",
    )(z, bw, cw, d, are, aim)
    shape = (batch, S5_GROUPS, S5_STATE)
    return y, sre.reshape(shape), sim.reshape(shape)


def _s5_sample(z, y_all, bw, cw, d, are, aim, h0_re, h0_im, row0, nb, nt):
    rows = nb * nt
    gn = S5_GROUPS * S5_STATE
    rb = row0 // rows
    nsb = S5_STEP_BLOCKS
    wspec = lambda shape: pl.BlockSpec((nsb,) + shape, lambda g: (g, 0, 0))
    vec = lambda w: pl.BlockSpec((1, nsb * w), lambda g: (0, g))
    st = pl.BlockSpec((nb, nsb * S5_SW), lambda g: (0, g))
    slab = pltpu.VMEM((nsb, rows, S5_CW), F32)
    y, sre, sim = pl.pallas_call(
        functools.partial(_s5_kernel, nseg=nb, seglen=nt, chained=False, ntc=1),
        grid=(S5_NBLK // nsb,),
        in_specs=[
            pl.BlockSpec((rows, nsb * S5_CW), lambda g: (rb, g)),
            wspec((S5_CW, 2 * S5_SW)),
            wspec((2 * S5_SW, S5_CW)),
            vec(S5_CW), vec(S5_SW), vec(S5_SW), st, st,
            pl.BlockSpec(memory_space=pl.ANY),
        ],
        out_specs=[pl.BlockSpec((rows, nsb * S5_CW), lambda g: (rb, g)), st, st],
        out_shape=[jax.ShapeDtypeStruct(y_all.shape, F32),
                   jax.ShapeDtypeStruct((nb, gn), F32),
                   jax.ShapeDtypeStruct((nb, gn), F32)],
        input_output_aliases={8: 0},
        scratch_shapes=[slab, slab,
                        pltpu.VMEM((nsb, rows, 2 * S5_SW), F32),
                        slab],
        compiler_params=_params("parallel"),
        name="s5_sample",
    )(z, bw, cw, d, are, aim, h0_re.reshape(nb, gn), h0_im.reshape(nb, gn), y_all)
    shape = (nb, S5_GROUPS, S5_STATE)
    return y, sre.reshape(shape), sim.reshape(shape)


def _cumsum_rows(x, seq_len):
    ngroup = GLA_ROWS // SUBLANES
    y = x.reshape(ngroup, SUBLANES, LANES)
    r = lax.broadcasted_iota(jnp.int32, y.shape, 1)
    for d in (1, 2, 4):
        y = y + jnp.where(r >= d, pltpu.roll(y, d, axis=1), 0.0)
    if seq_len == SUBLANES:
        return y.reshape(GLA_ROWS, LANES)
    parts, acc = [], None
    for g in range(ngroup):
        if g % (seq_len // SUBLANES) == 0:
            acc = None
        part = y[g] if acc is None else y[g] + acc
        parts.append(part)
        acc = part[SUBLANES - 1:SUBLANES, :]
    return jnp.concatenate(parts, axis=0)


def _level_ref(b, level):
    half = 1 << level
    span = 2 * half
    if span >= SUBLANES:
        b3 = b.reshape(GLA_ROWS // span, span, LANES)
        return jnp.broadcast_to(b3[:, half - 1:half, :], b3.shape).reshape(GLA_ROWS, LANES)
    b3 = b.reshape(GLA_ROWS // SUBLANES, SUBLANES, LANES)
    r = lax.broadcasted_iota(jnp.int32, b3.shape, 1)
    nspan = SUBLANES // span
    ref = jnp.broadcast_to(b3[:, (nspan - 1) * span + half - 1:(nspan - 1) * span + half, :], b3.shape)
    for p in range(nspan - 2, -1, -1):
        piece = jnp.broadcast_to(b3[:, p * span + half - 1:p * span + half, :], b3.shape)
        ref = jnp.where(r < (p + 1) * span, piece, ref)
    return ref.reshape(GLA_ROWS, LANES)


def _gla_tables(seq_len):
    t = lax.broadcasted_iota(jnp.int32, (GLA_ROWS, GLA_ROWS), 0)
    s = lax.broadcasted_iota(jnp.int32, (GLA_ROWS, GLA_ROWS), 1)
    x = t ^ s
    level = jnp.zeros((GLA_ROWS, GLA_ROWS), jnp.int32)
    for k in range(1, 6):
        level = level + jnp.where(x >= (1 << k), 1, 0)
    level = jnp.where(s < t, level, jnp.where(s == t, GLA_DIAG, -1))
    rows = lax.broadcasted_iota(jnp.int32, (GLA_ROWS, LANES), 0)
    return level, rows


def _gla_block(q, fz, v, lb, states, tables, seq_len, transposed_state):
    level, rows = tables
    nseq = GLA_ROWS // seq_len
    nlev = seq_len.bit_length() - 1
    qc = jax.nn.silu(q)
    fg = lb + (1.0 - lb) * jax.nn.sigmoid(fz)
    kc = 1.0 - fg
    vb = v.astype(BF16)
    b = _cumsum_rows(jnp.log2(fg), seq_len)

    nt = (((1,), (1,)), ((), ()))
    scores = [lax.dot_general(qc.astype(BF16), kc.astype(BF16), nt, preferred_element_type=F32)]
    for lev in range(nlev):
        half = 1 << lev
        if lev == 0:
            upper = (rows & 1) == 1
            qh = jnp.where(upper, qc * fg, 0.0).astype(BF16)
            kh = jnp.where(upper, 0.0, kc).astype(BF16)
        elif half >= SUBLANES:
            shape3 = (GLA_ROWS // (2 * half), 2 * half, LANES)
            b3, q3, k3 = b.reshape(shape3), qc.reshape(shape3), kc.reshape(shape3)
            ref = b3[:, half - 1:half, :]
            zero = jnp.zeros((shape3[0], half, LANES), F32)
            qh = jnp.concatenate([zero, q3[:, half:, :] * jnp.exp2(b3[:, half:, :] - ref)], axis=1)
            kh = jnp.concatenate([k3[:, :half, :] * jnp.exp2(ref - b3[:, :half, :]), zero], axis=1)
            qh = qh.reshape(GLA_ROWS, LANES).astype(BF16)
            kh = kh.reshape(GLA_ROWS, LANES).astype(BF16)
        else:
            w = jnp.exp2(-jnp.abs(b - _level_ref(b, lev)))
            upper = ((rows >> lev) & 1) == 1
            qh = jnp.where(upper, qc * w, 0.0).astype(BF16)
            kh = jnp.where(upper, 0.0, kc * w).astype(BF16)
        scores.append(lax.dot_general(qh, kh, nt, preferred_element_type=F32))
    att_rows = []
    for g in range(GLA_ROWS // SUBLANES):
        rs = slice(g * SUBLANES, (g + 1) * SUBLANES)
        lv = level[rs]
        a = jnp.where(lv == GLA_DIAG, scores[0][rs], 0.0)
        for lev in range(nlev):
            if lev < 3 or (g >> (lev - 3)) & 1:
                a = jnp.where(lv == lev, scores[lev + 1][rs], a)
        att_rows.append(a)
    o = _dot(jnp.concatenate(att_rows, axis=0).astype(BF16), vb)

    b3 = b.reshape(nseq, seq_len, LANES)
    blast = jnp.broadcast_to(b3[:, seq_len - 1:seq_len, :], b3.shape).reshape(GLA_ROWS, LANES)
    qe = (qc * jnp.exp2(b)).astype(BF16)
    kd = (kc * jnp.exp2(blast - b)).astype(BF16)
    tn = (((0,), (0,)), ((), ()))
    o_inter, new_states = [], []
    for n in range(nseq):
        sl = slice(n * seq_len, (n + 1) * seq_len)
        s_n = states[n]
        e_row = jnp.exp2(blast[n * seq_len:n * seq_len + 1, :])
        if transposed_state:
            o_inter.append(lax.dot_general(qe[sl], s_n.astype(BF16), nt, preferred_element_type=F32))
            upd = lax.dot_general(vb[sl], kd[sl], tn, preferred_element_type=F32)
            new_states.append(e_row * s_n + upd)
        else:
            o_inter.append(_dot(qe[sl], s_n.astype(BF16)))
            upd = lax.dot_general(kd[sl], vb[sl], tn, preferred_element_type=F32)
            e_col = jnp.transpose(jnp.broadcast_to(e_row, (HGRN_DK, LANES)))
            new_states.append(e_col * s_n + upd)
    o = o + (o_inter[0] if nseq == 1 else jnp.concatenate(o_inter, axis=0))
    return o, new_states


def _lower_bound(lbp, layer):
    if layer == 0:
        return jnp.zeros((1, lbp.shape[1]), F32)
    e = jnp.exp(lbp - jnp.max(lbp, axis=0, keepdims=True))
    p = e / jnp.sum(e, axis=0, keepdims=True)
    return jnp.sum(p[1:layer + 1, :], axis=0, keepdims=True)


def _gla_finish(o, g, gain):
    o = o * lax.rsqrt(jnp.mean(o * o, axis=-1, keepdims=True) + EPS)
    return o * gain * jax.nn.silu(g)


def _hgrn_prompt_kernel(q_ref, f_ref, v_ref, g_ref, lbp_ref, gain_ref, o_ref, sout_ref, s_ref,
                        *, layer, nchunk, ntb):
    tb = pl.program_id(2)

    @pl.when(tb == 0)
    def _():
        s_ref[...] = jnp.zeros_like(s_ref)

    tables = _gla_tables(GLA_ROWS)
    lb = _lower_bound(lbp_ref[...], layer)
    gain = gain_ref[layer:layer + 1, :]

    def chunk(c, _):
        r = pl.ds(pl.multiple_of(c * GLA_ROWS, GLA_ROWS), GLA_ROWS)
        for hh in range(HGRN_HB):
            cs = slice(hh * LANES, (hh + 1) * LANES)
            o, (s_new,) = _gla_block(q_ref[r, cs], f_ref[r, cs], v_ref[r, cs], lb[:, cs],
                                     [s_ref[hh]], tables, GLA_ROWS, True)
            s_ref[hh] = s_new
            o_ref[r, cs] = _gla_finish(o, g_ref[r, cs], gain[:, cs])
        return 0

    lax.fori_loop(0, nchunk, chunk, 0, unroll=2)

    @pl.when(tb == ntb - 1)
    def _():
        for hh in range(HGRN_HB):
            sout_ref[hh] = jnp.transpose(s_ref[hh])


def _hgrn_prompt(z, lbp, gain, layer, batch, seq):
    tb_rows = 512
    ntb = seq // tb_rows
    wb = HGRN_HB * LANES
    nhb = HGRN_HEADS // HGRN_HB

    def zspec(k):
        c0 = (S5_WIDTH + k * HGRN_WIDTH) // wb
        return pl.BlockSpec((tb_rows, wb), lambda b, h, t: (b * ntb + t, c0 + h))

    par = pl.BlockSpec((lbp.shape[0], wb), lambda b, h, t: (0, h))
    return pl.pallas_call(
        functools.partial(_hgrn_prompt_kernel, layer=layer, nchunk=tb_rows // GLA_ROWS, ntb=ntb),
        grid=(batch, nhb, ntb),
        in_specs=[zspec(0), zspec(1), zspec(2), zspec(3), par, par],
        out_specs=[pl.BlockSpec((tb_rows, wb), lambda b, h, t: (b * ntb + t, h)),
                   pl.BlockSpec((None, HGRN_HB, HGRN_DK, HGRN_DV), lambda b, h, t: (b, h, 0, 0))],
        out_shape=[jax.ShapeDtypeStruct((z.shape[0], HGRN_WIDTH), F32),
                   jax.ShapeDtypeStruct((batch, HGRN_HEADS, HGRN_DK, HGRN_DV), F32)],
        scratch_shapes=[pltpu.VMEM((HGRN_HB, HGRN_DK, HGRN_DV), F32)],
        compiler_params=_params("parallel", "parallel", "arbitrary"),
        name="hgrn_prompt",
    )(z, z, z, z, lbp, gain)


def _hgrn_sample_kernel(q_ref, f_ref, v_ref, g_ref, lbp_ref, gain_ref, s0_ref, *rest, layer, nt):
    o_ref, sout_ref = rest[-2:]
    nseq = GLA_ROWS // nt
    tables = _gla_tables(nt)
    lb = _lower_bound(lbp_ref[...], layer)
    gain = gain_ref[layer:layer + 1, :]
    for hh in range(HGRN_HB):
        cs = slice(hh * LANES, (hh + 1) * LANES)
        o, new_states = _gla_block(q_ref[:, cs], f_ref[:, cs], v_ref[:, cs], lb[:, cs],
                                   [s0_ref[n, hh] for n in range(nseq)], tables, nt, False)
        for n in range(nseq):
            sout_ref[n, hh] = new_states[n].astype(sout_ref.dtype)
        o_ref[:, cs] = _gla_finish(o, g_ref[:, cs], gain[:, cs])


def _hgrn_sample(z, y_all, lbp, gain, s0, s_all, layer, row0, nb, nt):
    nseq = GLA_ROWS // nt
    rb0 = row0 // GLA_ROWS
    wb = HGRN_HB * LANES

    def zspec(k):
        c0 = (S5_WIDTH + k * HGRN_WIDTH) // wb
        return pl.BlockSpec((GLA_ROWS, wb), lambda h, i: (rb0 + i, c0 + h))

    par = pl.BlockSpec((lbp.shape[0], wb), lambda h, i: (0, h))
    sspec = pl.BlockSpec((None, nseq, HGRN_HB, HGRN_DK, HGRN_DV), lambda h, i: (layer, i, h, 0, 0))
    anyspec = pl.BlockSpec(memory_space=pl.ANY)
    carried = (y_all,) if s_all is None else (y_all, s_all)
    aliases = {7: 0} if s_all is None else {7: 0, 8: 1}
    return pl.pallas_call(
        functools.partial(_hgrn_sample_kernel, layer=layer, nt=nt),
        grid=(HGRN_HEADS // HGRN_HB, nb // nseq),
        in_specs=[zspec(0), zspec(1), zspec(2), zspec(3), par, par, sspec] + [anyspec] * len(carried),
        out_specs=[pl.BlockSpec((GLA_ROWS, wb), lambda h, i: (rb0 + i, h)), sspec],
        out_shape=[jax.ShapeDtypeStruct(y_all.shape, F32),
                   jax.ShapeDtypeStruct(s0.shape, s0.dtype)],
        input_output_aliases=aliases,
        compiler_params=_params("parallel", "parallel"),
        name="hgrn_sample",
    )(z, z, z, z, lbp, gain, s0, *carried)


def _mixout_kernel(h_ref, ys_ref, yh_ref, wglu_ref, bglu_ref, wo_ref, gpost_ref, o_ref):
    ys = ys_ref[...]
    gate = jax.nn.sigmoid(_dot(ys.astype(BF16), wglu_ref[...]) + bglu_ref[...])
    out = (_dot((ys * gate).astype(BF16), wo_ref[:S5_WIDTH, :])
           + _dot(yh_ref[...].astype(BF16), wo_ref[S5_WIDTH:, :]))
    o_ref[...] = h_ref[...] + _rms(out, gpost_ref[...])


def _mixout(h, ys, yh, w_glu, b_glu, w_out, gpost):
    m = h.shape[0]
    tm = _row_tile(m)
    full = lambda shape: pl.BlockSpec(shape, lambda i: (0, 0))
    return pl.pallas_call(
        _mixout_kernel,
        grid=(m // tm,),
        in_specs=[
            pl.BlockSpec((tm, D_MODEL), lambda i: (i, 0)),
            pl.BlockSpec((tm, S5_WIDTH), lambda i: (i, 0)),
            pl.BlockSpec((tm, HGRN_WIDTH), lambda i: (i, 0)),
            full((S5_WIDTH, S5_WIDTH)), full((1, S5_WIDTH)),
            full((D_MODEL, D_MODEL)), full((1, D_MODEL)),
        ],
        out_specs=pl.BlockSpec((tm, D_MODEL), lambda i: (i, 0)),
        out_shape=jax.ShapeDtypeStruct((m, D_MODEL), F32),
        compiler_params=_params("parallel"),
        name="mixout",
    )(h, ys, yh, w_glu, b_glu, w_out, gpost)


def kernel(x_prompt, x_sample, state_s5_re, state_s5_im, state_hgrn, norm_pre, norm_post, ffn1_w_gate, ffn1_w_up, ffn1_w_down, ffn2_w_gate, ffn2_w_up, ffn2_w_down, w_in, w_out, s5_lam_re, s5_lam_im, s5_log_dt, s5_b_re, s5_b_im, s5_c_re, s5_c_im, s5_d, s5_w_glu, s5_b_glu, hgrn_lb, hgrn_norm):
    batch, seq, _ = x_prompt.shape
    nb, nt, _ = x_sample.shape
    depth = norm_pre.shape[0]
    mp = batch * seq
    sdt = state_hgrn.dtype

    h = jnp.concatenate([x_prompt.reshape(mp, D_MODEL), x_sample.reshape(nb * nt, D_MODEL)], axis=0)

    are, aim, bb_re, bb_im = _s5_discretise(s5_lam_re, s5_lam_im, s5_log_dt, s5_b_re, s5_b_im)
    bw, cw = _s5_block_weights(bb_re, bb_im, s5_c_re, s5_c_im)

    vec = lambda a: a.reshape(1, -1)

    small = ("re_p", "im_p", "h_p", "re_s", "im_s")
    outs = {k: [] for k in small}
    h_s_all = None
    for l in range(depth):
        h = _ffn(h, vec(norm_pre[l, 0]), vec(norm_post[l, 0]),
                 ffn1_w_gate, ffn1_w_up, ffn1_w_down, l)

        z = _mixin(h, vec(norm_pre[l, 1]), w_in, l)
        d = vec(s5_d[l])
        ys, re_p, im_p = _s5_prompt(z, bw[l], cw[l], d, are[l], aim[l], batch, seq)
        ys, re_s, im_s = _s5_sample(z, ys, bw[l], cw[l], d, are[l], aim[l],
                                    state_s5_re[l], state_s5_im[l], mp, nb, nt)
        yh, h_p = _hgrn_prompt(z, hgrn_lb, hgrn_norm, l, batch, seq)
        yh, h_s_all = _hgrn_sample(z, yh, hgrn_lb, hgrn_norm, state_hgrn, h_s_all, l, mp, nb, nt)
        h = _mixout(h, ys, yh, s5_w_glu[l].astype(BF16), vec(s5_b_glu[l]), w_out[l].astype(BF16),
                    vec(norm_post[l, 1]))

        h = _ffn(h, vec(norm_pre[l, 2]), vec(norm_post[l, 2]),
                 ffn2_w_gate, ffn2_w_up, ffn2_w_down, l)
        for k, a in zip(small, (re_p, im_p, h_p, re_s, im_s)):
            outs[k].append(a)

    stack = lambda k: jnp.stack(outs[k]).astype(sdt)
    return (h[:mp].reshape(batch, seq, D_MODEL), h[mp:].reshape(nb, nt, D_MODEL),
            stack("re_p"), stack("im_p"), stack("h_p"),
            stack("re_s"), stack("im_s"), h_s_all)
```

```python
import functools

import jax
import jax.numpy as jnp
from jax import lax
from jax.experimental import pallas as pl
from jax.experimental.pallas import tpu as pltpu

F32 = jnp.float32
BF16 = jnp.bfloat16

D_MODEL = 2048
S5_WIDTH = 1024
S5_GROUP = 16
S5_GROUPS = 64
S5_STATE = 64
HGRN_WIDTH = 1024
HGRN_DK = 128
HGRN_DV = 128
HGRN_HEADS = 8
D_FF = 5504
IN_WIDTH = S5_WIDTH + 4 * HGRN_WIDTH
EPS = 1e-6

LANES = 128
SUBLANES = 8
VMEM_LIMIT_BYTES = 60 * 1024 * 1024

TOKEN_TILE = 1024
FF_TILE = 256
MIXIN_ROWS = 1536
IN_TILE = 512
NORM_ROWS = 256
HGRN_HB = 8
S5_GB = 8
S5_NBLK = S5_GROUPS // S5_GB
S5_STEP_BLOCKS = 4
S5_SW = S5_GB * S5_STATE
S5_CW = S5_GB * S5_GROUP
GLA_ROWS = 64
GLA_DIAG = 7


def _params(*sem):
    return pltpu.CompilerParams(dimension_semantics=sem, vmem_limit_bytes=VMEM_LIMIT_BYTES)


def _row_tile(m, largest=512):
    for t in (1024, 512, 256, 128, 64):
        if t <= largest and m % t == 0:
            return t
    raise ValueError(f"token count {m} must be a multiple of 64")


def _rms(x, gain):
    return x * lax.rsqrt(jnp.mean(x * x, axis=-1, keepdims=True) + EPS) * gain


def _dot(a, b):
    return jnp.dot(a, b, preferred_element_type=F32)


def _for_row_chunks(nrows, fn, rows_per_slice=NORM_ROWS):
    step = min(rows_per_slice, nrows)
    for c in range(nrows // step):
        fn(pl.ds(c * step, step))


def _ffn_kernel(x_ref, gpre_ref, gpost_ref, wg_ref, wu_ref, wd_ref, o_ref, xn_ref, *, nj):
    j = pl.program_id(1)
    nrows = x_ref.shape[0]
    nvalid_last = D_FF - (nj - 1) * FF_TILE

    def weights(last):
        wd = wd_ref[...]
        if last:
            row = lax.broadcasted_iota(jnp.int32, wd.shape, 0)
            wd = jnp.where(row < nvalid_last, wd, 0.0)
        return wg_ref[...].astype(BF16), wu_ref[...].astype(BF16), wd.astype(BF16)

    def swiglu(r, w, last):
        xn = xn_ref[r, :]
        a = jax.nn.silu(_dot(xn, w[0])) * _dot(xn, w[1])
        if last:
            col = lax.broadcasted_iota(jnp.int32, a.shape, 1)
            a = jnp.where(col < nvalid_last, a, 0.0)
        return _dot(a.astype(BF16), w[2])

    @pl.when(j == 0)
    def _():
        w = weights(False)

        def first(r):
            xn_ref[r, :] = _rms(x_ref[r, :], gpre_ref[...]).astype(BF16)
            o_ref[r, :] = swiglu(r, w, False)

        _for_row_chunks(nrows, first)

    @pl.when((j > 0) & (j < nj - 1))
    def _():
        w = weights(False)

        def middle(r):
            o_ref[r, :] += swiglu(r, w, False)

        _for_row_chunks(nrows, middle, nrows)

    @pl.when(j == nj - 1)
    def _():
        w = weights(True)

        def last(r):
            acc = o_ref[r, :] + swiglu(r, w, True)
            o_ref[r, :] = x_ref[r, :] + 0.5 * _rms(acc, gpost_ref[...])

        _for_row_chunks(nrows, last)


def _ffn(h, gpre, gpost, wg, wu, wd, layer):
    m = h.shape[0]
    tm = _row_tile(m, TOKEN_TILE)
    nj = pl.cdiv(D_FF, FF_TILE)
    assert nj >= 2
    return pl.pallas_call(
        functools.partial(_ffn_kernel, nj=nj),
        grid=(m // tm, nj),
        in_specs=[
            pl.BlockSpec((tm, D_MODEL), lambda i, j: (i, 0)),
            pl.BlockSpec((1, D_MODEL), lambda i, j: (0, 0)),
            pl.BlockSpec((1, D_MODEL), lambda i, j: (0, 0)),
            pl.BlockSpec((None, D_MODEL, FF_TILE), lambda i, j: (layer, 0, j)),
            pl.BlockSpec((None, D_MODEL, FF_TILE), lambda i, j: (layer, 0, j)),
            pl.BlockSpec((None, FF_TILE, D_MODEL), lambda i, j: (layer, j, 0)),
        ],
        out_specs=pl.BlockSpec((tm, D_MODEL), lambda i, j: (i, 0)),
        out_shape=jax.ShapeDtypeStruct((m, D_MODEL), F32),
        scratch_shapes=[pltpu.VMEM((tm, D_MODEL), BF16)],
        compiler_params=_params("parallel", "arbitrary"),
        name="ffn",
    )(h, gpre, gpost, wg, wu, wd)


def _mixin_kernel(x_ref, gpre_ref, w_ref, o_ref, xn_ref):
    @pl.when(pl.program_id(1) == 0)
    def _():
        def norm_in(r):
            xn_ref[r, :] = _rms(x_ref[r, :], gpre_ref[...]).astype(BF16)

        _for_row_chunks(x_ref.shape[0], norm_in)

    o_ref[...] = _dot(xn_ref[...], w_ref[...].astype(BF16))


def _mixin(h, gpre, w_in, layer):
    m = h.shape[0]
    tm = MIXIN_ROWS if m % MIXIN_ROWS == 0 else _row_tile(m, TOKEN_TILE)
    return pl.pallas_call(
        _mixin_kernel,
        grid=(m // tm, IN_WIDTH // IN_TILE),
        in_specs=[
            pl.BlockSpec((tm, D_MODEL), lambda i, j: (i, 0)),
            pl.BlockSpec((1, D_MODEL), lambda i, j: (0, 0)),
            pl.BlockSpec((None, D_MODEL, IN_TILE), lambda i, j: (layer, 0, j)),
        ],
        out_specs=pl.BlockSpec((tm, IN_TILE), lambda i, j: (i, j)),
        out_shape=jax.ShapeDtypeStruct((m, IN_WIDTH), F32),
        scratch_shapes=[pltpu.VMEM((tm, D_MODEL), BF16)],
        compiler_params=_params("parallel", "arbitrary"),
        name="mixin",
    )(h, gpre, w_in)


def _s5_disc_kernel(lre_ref, lim_ref, ldt_ref, bre_ref, bim_ref,
                    are_ref, aim_ref, bbre_ref, bbim_ref):
    lam_re = lre_ref[...]
    lam_im = lim_ref[...]
    dt = jnp.exp(ldt_ref[...])
    mag = jnp.exp(lam_re * dt)
    ang = lam_im * dt
    abar_re = mag * jnp.cos(ang)
    abar_im = mag * jnp.sin(ang)
    p = abar_re - 1.0
    den = lam_re * lam_re + lam_im * lam_im
    z_re = (p * lam_re + abar_im * lam_im) / den
    z_im = (abar_im * lam_re - p * lam_im) / den
    are_ref[...] = abar_re
    aim_ref[...] = abar_im
    b_re = bre_ref[...]
    b_im = bim_ref[...]
    bbre_ref[...] = z_re * b_re - z_im * b_im
    bbim_ref[...] = z_re * b_im + z_im * b_re


def _s5_discretise(lam_re, lam_im, log_dt, b_re, b_im):
    depth = lam_re.shape[0]
    gn = S5_GROUPS * S5_STATE
    flat = lambda a: a.reshape(depth, 1, gn)
    ldt = jnp.broadcast_to(log_dt[:, :, None], lam_re.shape)
    chan_major = lambda b: jnp.transpose(b, (0, 3, 1, 2)).reshape(depth, S5_GROUP, gn)
    row = pl.BlockSpec((None, 1, gn), lambda l: (l, 0, 0))
    mat = pl.BlockSpec((None, S5_GROUP, gn), lambda l: (l, 0, 0))
    return pl.pallas_call(
        _s5_disc_kernel,
        grid=(depth,),
        in_specs=[row, row, row, mat, mat],
        out_specs=[row, row, mat, mat],
        out_shape=[jax.ShapeDtypeStruct((depth, 1, gn), F32)] * 2
        + [jax.ShapeDtypeStruct((depth, S5_GROUP, gn), F32)] * 2,
        compiler_params=_params("parallel"),
        name="s5_discretise",
    )(flat(lam_re), flat(lam_im), flat(ldt), chan_major(b_re), chan_major(b_im))


def _s5_block_weights(bb_re, bb_im, c_re, c_im):
    depth = bb_re.shape[0]
    eye = jnp.eye(S5_GB, dtype=F32)

    def in_map(bb):
        bb = bb.reshape(depth, S5_GROUP, S5_NBLK, S5_GB, S5_STATE)
        w = jnp.einsum("lcbgn,gh->lbgchn", bb, eye)
        return w.reshape(depth, S5_NBLK, S5_CW, S5_SW)

    def out_map(c):
        c = c.reshape(depth, S5_NBLK, S5_GB, S5_GROUP, S5_STATE)
        w = jnp.einsum("lbgcn,gh->lbgnhc", c, eye)
        return w.reshape(depth, S5_NBLK, S5_SW, S5_CW)

    bw = jnp.concatenate([in_map(bb_re), in_map(bb_im)], axis=-1).astype(BF16)
    cw = jnp.concatenate([out_map(c_re), -out_map(c_im)], axis=-2).astype(BF16)
    return bw, cw


def _s5_kernel(*refs, nseg, seglen, chained, ntc, nblk):
    if chained:
        (u_ref, bw_ref, cw_ref, d_ref, are_ref, aim_ref,
         y_ref, sre_ref, sim_ref, us_ref, up_ref, x_ref, yp_ref, carry_ref) = refs
    else:
        (u_ref, bw_ref, cw_ref, d_ref, are_ref, aim_ref, h0re_ref, h0im_ref, y_all_ref,
         y_ref, sre_ref, sim_ref, us_ref, up_ref, x_ref, yp_ref) = refs
    sw, cwid = S5_SW, S5_CW
    blocks = range(nblk)

    for k in blocks:
        us_ref[k] = u_ref[:, k * cwid:(k + 1) * cwid]

    def project_in(k, after=None):
        for i in range(seglen):
            rows = us_ref.at[k][pl.ds(i, nseg, stride=seglen), :]
            up_ref[k, i * nseg:(i + 1) * nseg, :] = rows if after is None else rows + after
        x_ref[k] = _dot(up_ref[k].astype(BF16), bw_ref[k])

    def project_out(k):
        y = _dot(x_ref[k].astype(BF16), cw_ref[k]) + d_ref[k] * up_ref[k]
        yp_ref[k] = jax.nn.gelu(y)
        for i in range(seglen):
            us_ref.at[k][pl.ds(i, nseg, stride=seglen), :] = yp_ref[k, i * nseg:(i + 1) * nseg, :]

    def make_step(k, store):
        arb = jnp.broadcast_to(are_ref[k], (SUBLANES, sw))
        aib = jnp.broadcast_to(aim_ref[k], (SUBLANES, sw))

        def step(xr, xi, r0):
            br = x_ref[k, pl.ds(r0, SUBLANES), :sw]
            bi = x_ref[k, pl.ds(r0, SUBLANES), sw:]
            xr, xi = arb * xr - aib * xi + br, arb * xi + aib * xr + bi
            if store:
                x_ref[k, pl.ds(r0, SUBLANES), :sw] = xr
                x_ref[k, pl.ds(r0, SUBLANES), sw:] = xi
            return xr, xi

        return step

    if chained:
        tc = pl.program_id(2)

        @pl.when(tc == 0)
        def _():
            carry_ref[...] = jnp.zeros_like(carry_ref)

        def scan(k, after=None):
            step = make_step(k, False)
            er = ei = jnp.zeros((SUBLANES, sw), F32)
            if after is not None:
                er = er + jnp.tile(after, (1, sw // LANES))
            for i in range(seglen):
                er, ei = step(er, ei, i * SUBLANES)
            pr, pi = are_ref[k], aim_ref[k]
            for _ in range(seglen.bit_length() - 1):
                pr, pi = pr * pr - pi * pi, 2.0 * pr * pi
            cr = carry_ref[k, 0:1, :sw]
            ci = carry_ref[k, 0:1, sw:]
            starts_r, starts_i = [], []
            for j in range(SUBLANES):
                starts_r.append(cr)
                starts_i.append(ci)
                cr, ci = (pr * cr - pi * ci + er[j:j + 1, :],
                          pr * ci + pi * cr + ei[j:j + 1, :])
            carry_ref[k, 0:1, :sw] = cr
            carry_ref[k, 0:1, sw:] = ci
            step = make_step(k, True)
            xr = jnp.concatenate(starts_r, axis=0)
            xi = jnp.concatenate(starts_i, axis=0)
            for i in range(seglen):
                xr, xi = step(xr, xi, i * SUBLANES)
            bits = pltpu.bitcast(xr[:, :LANES], jnp.uint32)
            return pltpu.bitcast(lax.shift_right_logical(bits, jnp.uint32(32)), F32)

        done = {}
        project_in(0)
        project_in(1)
        done[0] = scan(0)
        for t in range(1, nblk - 1):
            project_in(t + 1, done[t - 1])
            done[t] = scan(t, done[t - 1])
            project_out(t - 1)
        scan(nblk - 1, done[nblk - 2])
        project_out(nblk - 2)
        project_out(nblk - 1)

        @pl.when(tc == ntc - 1)
        def _():
            for k in blocks:
                sre_ref[:, k * sw:(k + 1) * sw] = carry_ref[k, 0:1, :sw]
                sim_ref[:, k * sw:(k + 1) * sw] = carry_ref[k, 0:1, sw:]
    else:
        for k in blocks:
            project_in(k)

        def scan_group(sg, _):
            g0 = pl.multiple_of(sg * SUBLANES, SUBLANES)
            for k in blocks:
                step = make_step(k, True)
                xr = h0re_ref[pl.ds(g0, SUBLANES), k * sw:(k + 1) * sw]
                xi = h0im_ref[pl.ds(g0, SUBLANES), k * sw:(k + 1) * sw]
                for i in range(seglen):
                    xr, xi = step(xr, xi, pl.multiple_of(i * nseg + g0, SUBLANES))
                sre_ref[pl.ds(g0, SUBLANES), k * sw:(k + 1) * sw] = xr
                sim_ref[pl.ds(g0, SUBLANES), k * sw:(k + 1) * sw] = xi
            return 0

        lax.fori_loop(0, nseg // SUBLANES, scan_group, 0)
        for k in blocks:
            project_out(k)

    for k in blocks:
        y_ref[:, k * cwid:(k + 1) * cwid] = us_ref[k]


def _s5_prompt(z, bw, cw, d, are, aim, batch, seq):
    nseg, seglen = SUBLANES, 64
    rows = nseg * seglen
    ntc = seq // rows
    gn = S5_GROUPS * S5_STATE
    nsb = S5_NBLK
    wspec = lambda shape: pl.BlockSpec((nsb,) + shape, lambda b, g, t: (g, 0, 0))
    vec = lambda w: wspec((1, w))
    st = pl.BlockSpec((None, 1, nsb * S5_SW), lambda b, g, t: (b, 0, g))
    slab = pltpu.VMEM((nsb, rows, S5_CW), F32)
    y, sre, sim = pl.pallas_call(
        functools.partial(_s5_kernel, nseg=nseg, seglen=seglen, chained=True, ntc=ntc, nblk=nsb),
        grid=(batch, S5_NBLK // nsb, ntc),
        in_specs=[
            pl.BlockSpec((rows, nsb * S5_CW), lambda b, g, t: (b * ntc + t, g)),
            wspec((S5_CW, 2 * S5_SW)),
            wspec((2 * S5_SW, S5_CW)),
            vec(S5_CW), vec(S5_SW), vec(S5_SW),
        ],
        out_specs=[pl.BlockSpec((rows, nsb * S5_CW), lambda b, g, t: (b * ntc + t, g)), st, st],
        out_shape=[jax.ShapeDtypeStruct((z.shape[0], S5_WIDTH), F32),
                   jax.ShapeDtypeStruct((batch, 1, gn), F32),
                   jax.ShapeDtypeStruct((batch, 1, gn), F32)],
        scratch_shapes=[slab, slab,
                        pltpu.VMEM((nsb, rows, 2 * S5_SW), F32),
                        slab,
                        pltpu.VMEM((nsb, SUBLANES, 2 * S5_SW), F32)],
        compiler_params=_params("parallel", "parallel", "arbitrary"),
        name="s5_prompt",
    )(z, bw, cw, d.reshape(S5_NBLK, 1, S5_CW), are.reshape(S5_NBLK, 1, S5_SW),
      aim.reshape(S5_NBLK, 1, S5_SW))
    shape = (batch, S5_GROUPS, S5_STATE)
    return y, sre.reshape(shape), sim.reshape(shape)


def _s5_sample(z, y_all, bw, cw, d, are, aim, h0_re, h0_im, row0, nb, nt):
    rows = nb * nt
    gn = S5_GROUPS * S5_STATE
    rb = row0 // rows
    nsb = S5_STEP_BLOCKS
    wspec = lambda shape: pl.BlockSpec((nsb,) + shape, lambda g: (g, 0, 0))
    vec = lambda w: wspec((1, w))
    st = pl.BlockSpec((nb, nsb * S5_SW), lambda g: (0, g))
    slab = pltpu.VMEM((nsb, rows, S5_CW), F32)
    y, sre, sim = pl.pallas_call(
        functools.partial(_s5_kernel, nseg=nb, seglen=nt, chained=False, ntc=1, nblk=nsb),
        grid=(S5_NBLK // nsb,),
        in_specs=[
            pl.BlockSpec((rows, nsb * S5_CW), lambda g: (rb, g)),
            wspec((S5_CW, 2 * S5_SW)),
            wspec((2 * S5_SW, S5_CW)),
            vec(S5_CW), vec(S5_SW), vec(S5_SW), st, st,
            pl.BlockSpec(memory_space=pl.ANY),
        ],
        out_specs=[pl.BlockSpec((rows, nsb * S5_CW), lambda g: (rb, g)), st, st],
        out_shape=[jax.ShapeDtypeStruct(y_all.shape, F32),
                   jax.ShapeDtypeStruct((nb, gn), F32),
                   jax.ShapeDtypeStruct((nb, gn), F32)],
        input_output_aliases={8: 0},
        scratch_shapes=[slab, slab,
                        pltpu.VMEM((nsb, rows, 2 * S5_SW), F32),
                        slab],
        compiler_params=_params("parallel"),
        name="s5_sample",
    )(z, bw, cw, d.reshape(S5_NBLK, 1, S5_CW), are.reshape(S5_NBLK, 1, S5_SW),
      aim.reshape(S5_NBLK, 1, S5_SW), h0_re.reshape(nb, gn), h0_im.reshape(nb, gn), y_all)
    shape = (nb, S5_GROUPS, S5_STATE)
    return y, sre.reshape(shape), sim.reshape(shape)


def _cumsum_rows(x, seq_len):
    ngroup = GLA_ROWS // SUBLANES
    y = x.reshape(ngroup, SUBLANES, LANES)
    r = lax.broadcasted_iota(jnp.int32, y.shape, 1)
    for d in (1, 2, 4):
        y = y + jnp.where(r >= d, pltpu.roll(y, d, axis=1), 0.0)
    if seq_len == SUBLANES:
        return y.reshape(GLA_ROWS, LANES)
    parts, acc = [], None
    for g in range(ngroup):
        if g % (seq_len // SUBLANES) == 0:
            acc = None
        part = y[g] if acc is None else y[g] + acc
        parts.append(part)
        acc = part[SUBLANES - 1:SUBLANES, :]
    return jnp.concatenate(parts, axis=0)


def _level_ref(b, level):
    half = 1 << level
    span = 2 * half
    if span >= SUBLANES:
        b3 = b.reshape(GLA_ROWS // span, span, LANES)
        return jnp.broadcast_to(b3[:, half - 1:half, :], b3.shape).reshape(GLA_ROWS, LANES)
    b3 = b.reshape(GLA_ROWS // SUBLANES, SUBLANES, LANES)
    r = lax.broadcasted_iota(jnp.int32, b3.shape, 1)
    nspan = SUBLANES // span
    ref = jnp.broadcast_to(b3[:, (nspan - 1) * span + half - 1:(nspan - 1) * span + half, :], b3.shape)
    for p in range(nspan - 2, -1, -1):
        piece = jnp.broadcast_to(b3[:, p * span + half - 1:p * span + half, :], b3.shape)
        ref = jnp.where(r < (p + 1) * span, piece, ref)
    return ref.reshape(GLA_ROWS, LANES)


def _gla_tables(seq_len):
    t = lax.broadcasted_iota(jnp.int32, (GLA_ROWS, GLA_ROWS), 0)
    s = lax.broadcasted_iota(jnp.int32, (GLA_ROWS, GLA_ROWS), 1)
    x = t ^ s
    level = jnp.zeros((GLA_ROWS, GLA_ROWS), jnp.int32)
    for k in range(1, 6):
        level = level + jnp.where(x >= (1 << k), 1, 0)
    level = jnp.where(s < t, level, jnp.where(s == t, GLA_DIAG, -1))
    rows = lax.broadcasted_iota(jnp.int32, (GLA_ROWS, LANES), 0)
    return level, rows


def _gla_block(q, fz, v, lb, states, tables, seq_len, transposed_state):
    level, rows = tables
    nseq = GLA_ROWS // seq_len
    nlev = seq_len.bit_length() - 1
    qc = jax.nn.silu(q)
    fg = lb + (1.0 - lb) * jax.nn.sigmoid(fz)
    kc = 1.0 - fg
    vb = v.astype(BF16)
    b = _cumsum_rows(jnp.log2(fg), seq_len)

    nt = (((1,), (1,)), ((), ()))
    scores = [lax.dot_general(qc.astype(BF16), kc.astype(BF16), nt, preferred_element_type=F32)]
    for lev in range(nlev):
        half = 1 << lev
        if lev == 0:
            upper = (rows & 1) == 1
            qh = jnp.where(upper, qc * fg, 0.0).astype(BF16)
            kh = jnp.where(upper, 0.0, kc).astype(BF16)
        elif half >= SUBLANES:
            shape3 = (GLA_ROWS // (2 * half), 2 * half, LANES)
            b3, q3, k3 = b.reshape(shape3), qc.reshape(shape3), kc.reshape(shape3)
            ref = b3[:, half - 1:half, :]
            zero = jnp.zeros((shape3[0], half, LANES), F32)
            qh = jnp.concatenate([zero, q3[:, half:, :] * jnp.exp2(b3[:, half:, :] - ref)], axis=1)
            kh = jnp.concatenate([k3[:, :half, :] * jnp.exp2(ref - b3[:, :half, :]), zero], axis=1)
            qh = qh.reshape(GLA_ROWS, LANES).astype(BF16)
            kh = kh.reshape(GLA_ROWS, LANES).astype(BF16)
        else:
            w = jnp.exp2(-jnp.abs(b - _level_ref(b, lev)))
            upper = ((rows >> lev) & 1) == 1
            qh = jnp.where(upper, qc * w, 0.0).astype(BF16)
            kh = jnp.where(upper, 0.0, kc * w).astype(BF16)
        scores.append(lax.dot_general(qh, kh, nt, preferred_element_type=F32))
    att_rows = []
    for g in range(GLA_ROWS // SUBLANES):
        rs = slice(g * SUBLANES, (g + 1) * SUBLANES)
        lv = level[rs]
        a = jnp.where(lv == GLA_DIAG, scores[0][rs], 0.0)
        for lev in range(nlev):
            if lev < 3 or (g >> (lev - 3)) & 1:
                a = jnp.where(lv == lev, scores[lev + 1][rs], a)
        att_rows.append(a)
    o = _dot(jnp.concatenate(att_rows, axis=0).astype(BF16), vb)

    b3 = b.reshape(nseq, seq_len, LANES)
    blast = jnp.broadcast_to(b3[:, seq_len - 1:seq_len, :], b3.shape).reshape(GLA_ROWS, LANES)
    qe = (qc * jnp.exp2(b)).astype(BF16)
    kd = (kc * jnp.exp2(blast - b)).astype(BF16)
    tn = (((0,), (0,)), ((), ()))
    o_inter, new_states = [], []
    for n in range(nseq):
        sl = slice(n * seq_len, (n + 1) * seq_len)
        s_n = states[n]
        e_row = jnp.exp2(blast[n * seq_len:n * seq_len + 1, :])
        if transposed_state:
            o_inter.append(lax.dot_general(qe[sl], s_n.astype(BF16), nt, preferred_element_type=F32))
            upd = lax.dot_general(vb[sl], kd[sl], tn, preferred_element_type=F32)
            new_states.append(e_row * s_n + upd)
        else:
            o_inter.append(_dot(qe[sl], s_n.astype(BF16)))
            upd = lax.dot_general(kd[sl], vb[sl], tn, preferred_element_type=F32)
            e_col = jnp.transpose(jnp.broadcast_to(e_row, (HGRN_DK, LANES)))
            new_states.append(e_col * s_n + upd)
    o = o + (o_inter[0] if nseq == 1 else jnp.concatenate(o_inter, axis=0))
    return o, new_states


def _lower_bound(lbp, layer):
    if layer == 0:
        return jnp.zeros((1, lbp.shape[1]), F32)
    e = jnp.exp(lbp - jnp.max(lbp, axis=0, keepdims=True))
    p = e / jnp.sum(e, axis=0, keepdims=True)
    return jnp.sum(p[1:layer + 1, :], axis=0, keepdims=True)


def _gla_finish(o, g, gain):
    o = o * lax.rsqrt(jnp.mean(o * o, axis=-1, keepdims=True) + EPS)
    return o * gain * jax.nn.silu(g)


def _hgrn_prompt_kernel(q_ref, f_ref, v_ref, g_ref, lbp_ref, gain_ref, o_ref, sout_ref, s_ref,
                        *, layer, nchunk, ntb):
    tb = pl.program_id(2)

    @pl.when(tb == 0)
    def _():
        s_ref[...] = jnp.zeros_like(s_ref)

    tables = _gla_tables(GLA_ROWS)
    lb = _lower_bound(lbp_ref[...], layer)
    gain = gain_ref[layer:layer + 1, :]

    def chunk(c, _):
        r = pl.ds(pl.multiple_of(c * GLA_ROWS, GLA_ROWS), GLA_ROWS)
        for hh in range(HGRN_HB):
            cs = slice(hh * LANES, (hh + 1) * LANES)
            o, (s_new,) = _gla_block(q_ref[r, cs], f_ref[r, cs], v_ref[r, cs], lb[:, cs],
                                     [s_ref[hh]], tables, GLA_ROWS, True)
            s_ref[hh] = s_new
            o_ref[r, cs] = _gla_finish(o, g_ref[r, cs], gain[:, cs])
        return 0

    lax.fori_loop(0, nchunk, chunk, 0, unroll=2)

    @pl.when(tb == ntb - 1)
    def _():
        for hh in range(HGRN_HB):
            sout_ref[hh] = jnp.transpose(s_ref[hh])


def _hgrn_prompt(z, lbp, gain, layer, batch, seq):
    tb_rows = 512
    ntb = seq // tb_rows
    wb = HGRN_HB * LANES
    nhb = HGRN_HEADS // HGRN_HB

    def zspec(k):
        c0 = (S5_WIDTH + k * HGRN_WIDTH) // wb
        return pl.BlockSpec((tb_rows, wb), lambda b, h, t: (b * ntb + t, c0 + h))

    par = pl.BlockSpec((lbp.shape[0], wb), lambda b, h, t: (0, h))
    return pl.pallas_call(
        functools.partial(_hgrn_prompt_kernel, layer=layer, nchunk=tb_rows // GLA_ROWS, ntb=ntb),
        grid=(batch, nhb, ntb),
        in_specs=[zspec(0), zspec(1), zspec(2), zspec(3), par, par],
        out_specs=[pl.BlockSpec((tb_rows, wb), lambda b, h, t: (b * ntb + t, h)),
                   pl.BlockSpec((None, HGRN_HB, HGRN_DK, HGRN_DV), lambda b, h, t: (b, h, 0, 0))],
        out_shape=[jax.ShapeDtypeStruct((z.shape[0], HGRN_WIDTH), F32),
                   jax.ShapeDtypeStruct((batch, HGRN_HEADS, HGRN_DK, HGRN_DV), F32)],
        scratch_shapes=[pltpu.VMEM((HGRN_HB, HGRN_DK, HGRN_DV), F32)],
        compiler_params=_params("parallel", "parallel", "arbitrary"),
        name="hgrn_prompt",
    )(z, z, z, z, lbp, gain)


def _hgrn_sample_kernel(q_ref, f_ref, v_ref, g_ref, lbp_ref, gain_ref, s0_ref, *rest, layer, nt):
    o_ref, sout_ref = rest[-2:]
    nseq = GLA_ROWS // nt
    tables = _gla_tables(nt)
    lb = _lower_bound(lbp_ref[...], layer)
    gain = gain_ref[layer:layer + 1, :]
    for hh in range(HGRN_HB):
        cs = slice(hh * LANES, (hh + 1) * LANES)
        o, new_states = _gla_block(q_ref[:, cs], f_ref[:, cs], v_ref[:, cs], lb[:, cs],
                                   [s0_ref[n, hh] for n in range(nseq)], tables, nt, False)
        for n in range(nseq):
            sout_ref[n, hh] = new_states[n].astype(sout_ref.dtype)
        o_ref[:, cs] = _gla_finish(o, g_ref[:, cs], gain[:, cs])


def _hgrn_sample(z, y_all, lbp, gain, s0, s_all, layer, row0, nb, nt):
    nseq = GLA_ROWS // nt
    rb0 = row0 // GLA_ROWS
    wb = HGRN_HB * LANES

    def zspec(k):
        c0 = (S5_WIDTH + k * HGRN_WIDTH) // wb
        return pl.BlockSpec((GLA_ROWS, wb), lambda h, i: (rb0 + i, c0 + h))

    par = pl.BlockSpec((lbp.shape[0], wb), lambda h, i: (0, h))
    sspec = pl.BlockSpec((None, nseq, HGRN_HB, HGRN_DK, HGRN_DV), lambda h, i: (layer, i, h, 0, 0))
    anyspec = pl.BlockSpec(memory_space=pl.ANY)
    carried = (y_all,) if s_all is None else (y_all, s_all)
    aliases = {7: 0} if s_all is None else {7: 0, 8: 1}
    return pl.pallas_call(
        functools.partial(_hgrn_sample_kernel, layer=layer, nt=nt),
        grid=(HGRN_HEADS // HGRN_HB, nb // nseq),
        in_specs=[zspec(0), zspec(1), zspec(2), zspec(3), par, par, sspec] + [anyspec] * len(carried),
        out_specs=[pl.BlockSpec((GLA_ROWS, wb), lambda h, i: (rb0 + i, h)), sspec],
        out_shape=[jax.ShapeDtypeStruct(y_all.shape, F32),
                   jax.ShapeDtypeStruct(s0.shape, s0.dtype)],
        input_output_aliases=aliases,
        compiler_params=_params("parallel", "parallel"),
        name="hgrn_sample",
    )(z, z, z, z, lbp, gain, s0, *carried)


def _mixout_kernel(h_ref, ys_ref, yh_ref, wglu_ref, bglu_ref, wo_ref, gpost_ref, o_ref):
    ys = ys_ref[...]
    gate = jax.nn.sigmoid(_dot(ys.astype(BF16), wglu_ref[...]) + bglu_ref[...])
    out = (_dot((ys * gate).astype(BF16), wo_ref[:S5_WIDTH, :])
           + _dot(yh_ref[...].astype(BF16), wo_ref[S5_WIDTH:, :]))
    o_ref[...] = h_ref[...] + _rms(out, gpost_ref[...])


def _mixout(h, ys, yh, w_glu, b_glu, w_out, gpost):
    m = h.shape[0]
    tm = _row_tile(m)
    full = lambda shape: pl.BlockSpec(shape, lambda i: (0, 0))
    return pl.pallas_call(
        _mixout_kernel,
        grid=(m // tm,),
        in_specs=[
            pl.BlockSpec((tm, D_MODEL), lambda i: (i, 0)),
            pl.BlockSpec((tm, S5_WIDTH), lambda i: (i, 0)),
            pl.BlockSpec((tm, HGRN_WIDTH), lambda i: (i, 0)),
            full((S5_WIDTH, S5_WIDTH)), full((1, S5_WIDTH)),
            full((D_MODEL, D_MODEL)), full((1, D_MODEL)),
        ],
        out_specs=pl.BlockSpec((tm, D_MODEL), lambda i: (i, 0)),
        out_shape=jax.ShapeDtypeStruct((m, D_MODEL), F32),
        compiler_params=_params("parallel"),
        name="mixout",
    )(h, ys, yh, w_glu, b_glu, w_out, gpost)


def kernel(x_prompt, x_sample, state_s5_re, state_s5_im, state_hgrn, norm_pre, norm_post, ffn1_w_gate, ffn1_w_up, ffn1_w_down, ffn2_w_gate, ffn2_w_up, ffn2_w_down, w_in, w_out, s5_lam_re, s5_lam_im, s5_log_dt, s5_b_re, s5_b_im, s5_c_re, s5_c_im, s5_d, s5_w_glu, s5_b_glu, hgrn_lb, hgrn_norm):
    batch, seq, _ = x_prompt.shape
    nb, nt, _ = x_sample.shape
    depth = norm_pre.shape[0]
    mp = batch * seq
    sdt = state_hgrn.dtype

    h = jnp.concatenate([x_prompt.reshape(mp, D_MODEL), x_sample.reshape(nb * nt, D_MODEL)], axis=0)

    are, aim, bb_re, bb_im = _s5_discretise(s5_lam_re, s5_lam_im, s5_log_dt, s5_b_re, s5_b_im)
    bw, cw = _s5_block_weights(bb_re, bb_im, s5_c_re, s5_c_im)

    vec = lambda a: a.reshape(1, -1)

    small = ("re_p", "im_p", "h_p", "re_s", "im_s")
    outs = {k: [] for k in small}
    h_s_all = None
    for l in range(depth):
        h = _ffn(h, vec(norm_pre[l, 0]), vec(norm_post[l, 0]),
                 ffn1_w_gate, ffn1_w_up, ffn1_w_down, l)

        z = _mixin(h, vec(norm_pre[l, 1]), w_in, l)
        d = vec(s5_d[l])
        ys, re_p, im_p = _s5_prompt(z, bw[l], cw[l], d, are[l], aim[l], batch, seq)
        ys, re_s, im_s = _s5_sample(z, ys, bw[l], cw[l], d, are[l], aim[l],
                                    state_s5_re[l], state_s5_im[l], mp, nb, nt)
        yh, h_p = _hgrn_prompt(z, hgrn_lb, hgrn_norm, l, batch, seq)
        yh, h_s_all = _hgrn_sample(z, yh, hgrn_lb, hgrn_norm, state_hgrn, h_s_all, l, mp, nb, nt)
        h = _mixout(h, ys, yh, s5_w_glu[l].astype(BF16), vec(s5_b_glu[l]), w_out[l].astype(BF16),
                    vec(norm_post[l, 1]))

        h = _ffn(h, vec(norm_pre[l, 2]), vec(norm_post[l, 2]),
                 ffn2_w_gate, ffn2_w_up, ffn2_w_down, l)
        for k, a in zip(small, (re_p, im_p, h_p, re_s, im_s)):
            outs[k].append(a)

    stack = lambda k: jnp.stack(outs[k]).astype(sdt)
    return (h[:mp].reshape(batch, seq, D_MODEL), h[mp:].reshape(nb, nt, D_MODEL),
            stack("re_p"), stack("im_p"), stack("h_p"),
            stack("re_s"), stack("im_s"), h_s_all)
```

```python
import functools

import jax
import jax.numpy as jnp
from jax import lax
from jax.experimental import pallas as pl
from jax.experimental.pallas import tpu as pltpu

F32 = jnp.float32
BF16 = jnp.bfloat16

D_MODEL = 2048
S5_WIDTH = 1024
S5_GROUP = 16
S5_GROUPS = 64
S5_STATE = 64
HGRN_WIDTH = 1024
HGRN_DK = 128
HGRN_DV = 128
HGRN_HEADS = 8
D_FF = 5504
IN_WIDTH = S5_WIDTH + 4 * HGRN_WIDTH
EPS = 1e-6

LANES = 128
SUBLANES = 8
VMEM_LIMIT_BYTES = 60 * 1024 * 1024

TOKEN_TILE = 1024
FF_TILE = 256
MIXIN_ROWS = 1536
IN_TILE = 512
NORM_ROWS = 256
HGRN_HB = 8
S5_GB = 8
S5_NBLK = S5_GROUPS // S5_GB
S5_STEP_BLOCKS = 4
S5_SW = S5_GB * S5_STATE
S5_CW = S5_GB * S5_GROUP
GLA_ROWS = 64
GLA_DIAG = 7


def _params(*sem):
    return pltpu.CompilerParams(dimension_semantics=sem, vmem_limit_bytes=VMEM_LIMIT_BYTES)


def _row_tile(m, largest=512):
    for t in (1024, 512, 256, 128, 64):
        if t <= largest and m % t == 0:
            return t
    raise ValueError(f"token count {m} must be a multiple of 64")


def _rms(x, gain):
    return x * lax.rsqrt(jnp.mean(x * x, axis=-1, keepdims=True) + EPS) * gain


def _dot(a, b):
    return jnp.dot(a, b, preferred_element_type=F32)


def _for_row_chunks(nrows, fn, rows_per_slice=NORM_ROWS):
    step = min(rows_per_slice, nrows)
    for c in range(nrows // step):
        fn(pl.ds(c * step, step))


def _ffn_kernel(x_ref, gpre_ref, gpost_ref, wg_ref, wu_ref, wd_ref, o_ref, xn_ref, *, nj):
    j = pl.program_id(1)
    nrows = x_ref.shape[0]
    nvalid_last = D_FF - (nj - 1) * FF_TILE

    def weights(last):
        wd = wd_ref[...]
        if last:
            row = lax.broadcasted_iota(jnp.int32, wd.shape, 0)
            wd = jnp.where(row < nvalid_last, wd, 0.0)
        return wg_ref[...].astype(BF16), wu_ref[...].astype(BF16), wd.astype(BF16)

    def swiglu(r, w, last):
        xn = xn_ref[r, :]
        a = jax.nn.silu(_dot(xn, w[0])) * _dot(xn, w[1])
        if last:
            col = lax.broadcasted_iota(jnp.int32, a.shape, 1)
            a = jnp.where(col < nvalid_last, a, 0.0)
        return _dot(a.astype(BF16), w[2])

    @pl.when(j == 0)
    def _():
        w = weights(False)

        def first(r):
            xn_ref[r, :] = _rms(x_ref[r, :], gpre_ref[...]).astype(BF16)
            o_ref[r, :] = swiglu(r, w, False)

        _for_row_chunks(nrows, first)

    @pl.when((j > 0) & (j < nj - 1))
    def _():
        w = weights(False)

        def middle(r):
            o_ref[r, :] += swiglu(r, w, False)

        _for_row_chunks(nrows, middle, nrows)

    @pl.when(j == nj - 1)
    def _():
        w = weights(True)

        def last(r):
            acc = o_ref[r, :] + swiglu(r, w, True)
            o_ref[r, :] = x_ref[r, :] + 0.5 * _rms(acc, gpost_ref[...])

        _for_row_chunks(nrows, last)


def _ffn(h, gpre, gpost, wg, wu, wd, layer):
    m = h.shape[0]
    tm = _row_tile(m, TOKEN_TILE)
    nj = pl.cdiv(D_FF, FF_TILE)
    assert nj >= 2
    return pl.pallas_call(
        functools.partial(_ffn_kernel, nj=nj),
        grid=(m // tm, nj),
        in_specs=[
            pl.BlockSpec((tm, D_MODEL), lambda i, j: (i, 0)),
            pl.BlockSpec((1, D_MODEL), lambda i, j: (0, 0)),
            pl.BlockSpec((1, D_MODEL), lambda i, j: (0, 0)),
            pl.BlockSpec((None, D_MODEL, FF_TILE), lambda i, j: (layer, 0, j)),
            pl.BlockSpec((None, D_MODEL, FF_TILE), lambda i, j: (layer, 0, j)),
            pl.BlockSpec((None, FF_TILE, D_MODEL), lambda i, j: (layer, j, 0)),
        ],
        out_specs=pl.BlockSpec((tm, D_MODEL), lambda i, j: (i, 0)),
        out_shape=jax.ShapeDtypeStruct((m, D_MODEL), F32),
        scratch_shapes=[pltpu.VMEM((tm, D_MODEL), BF16)],
        compiler_params=_params("parallel", "arbitrary"),
        name="ffn",
    )(h, gpre, gpost, wg, wu, wd)


def _mixin_kernel(x_ref, gpre_ref, w_ref, o_ref, xn_ref):
    j = pl.program_id(1)

    @pl.when(j == 0)
    def _():
        w = w_ref[...].astype(BF16)

        def first(r):
            xn_ref[r, :] = _rms(x_ref[r, :], gpre_ref[...]).astype(BF16)
            o_ref[r, :] = _dot(xn_ref[r, :], w)

        _for_row_chunks(x_ref.shape[0], first)

    @pl.when(j > 0)
    def _():
        o_ref[...] = _dot(xn_ref[...], w_ref[...].astype(BF16))


def _mixin(h, gpre, w_in, layer):
    m = h.shape[0]
    tm = MIXIN_ROWS if m % MIXIN_ROWS == 0 else _row_tile(m, TOKEN_TILE)
    return pl.pallas_call(
        _mixin_kernel,
        grid=(m // tm, IN_WIDTH // IN_TILE),
        in_specs=[
            pl.BlockSpec((tm, D_MODEL), lambda i, j: (i, 0)),
            pl.BlockSpec((1, D_MODEL), lambda i, j: (0, 0)),
            pl.BlockSpec((None, D_MODEL, IN_TILE), lambda i, j: (layer, 0, j)),
        ],
        out_specs=pl.BlockSpec((tm, IN_TILE), lambda i, j: (i, j)),
        out_shape=jax.ShapeDtypeStruct((m, IN_WIDTH), F32),
        scratch_shapes=[pltpu.VMEM((tm, D_MODEL), BF16)],
        compiler_params=_params("parallel", "arbitrary"),
        name="mixin",
    )(h, gpre, w_in)


def _s5_disc_kernel(lre_ref, lim_ref, ldt_ref, bre_ref, bim_ref,
                    are_ref, aim_ref, bbre_ref, bbim_ref):
    lam_re = lre_ref[...]
    lam_im = lim_ref[...]
    dt = jnp.exp(ldt_ref[...])
    mag = jnp.exp(lam_re * dt)
    ang = lam_im * dt
    abar_re = mag * jnp.cos(ang)
    abar_im = mag * jnp.sin(ang)
    p = abar_re - 1.0
    den = lam_re * lam_re + lam_im * lam_im
    z_re = (p * lam_re + abar_im * lam_im) / den
    z_im = (abar_im * lam_re - p * lam_im) / den
    are_ref[...] = abar_re
    aim_ref[...] = abar_im
    b_re = bre_ref[...]
    b_im = bim_ref[...]
    bbre_ref[...] = z_re * b_re - z_im * b_im
    bbim_ref[...] = z_re * b_im + z_im * b_re


def _s5_discretise(lam_re, lam_im, log_dt, b_re, b_im):
    depth = lam_re.shape[0]
    gn = S5_GROUPS * S5_STATE
    flat = lambda a: a.reshape(depth, 1, gn)
    ldt = jnp.broadcast_to(log_dt[:, :, None], lam_re.shape)
    chan_major = lambda b: jnp.transpose(b, (0, 3, 1, 2)).reshape(depth, S5_GROUP, gn)
    row = pl.BlockSpec((None, 1, gn), lambda l: (l, 0, 0))
    mat = pl.BlockSpec((None, S5_GROUP, gn), lambda l: (l, 0, 0))
    return pl.pallas_call(
        _s5_disc_kernel,
        grid=(depth,),
        in_specs=[row, row, row, mat, mat],
        out_specs=[row, row, mat, mat],
        out_shape=[jax.ShapeDtypeStruct((depth, 1, gn), F32)] * 2
        + [jax.ShapeDtypeStruct((depth, S5_GROUP, gn), F32)] * 2,
        compiler_params=_params("parallel"),
        name="s5_discretise",
    )(flat(lam_re), flat(lam_im), flat(ldt), chan_major(b_re), chan_major(b_im))


def _s5_block_weights(bb_re, bb_im, c_re, c_im):
    depth = bb_re.shape[0]
    eye = jnp.eye(S5_GB, dtype=F32)

    def in_map(bb):
        bb = bb.reshape(depth, S5_GROUP, S5_NBLK, S5_GB, S5_STATE)
        w = jnp.einsum("lcbgn,gh->lbgchn", bb, eye)
        return w.reshape(depth, S5_NBLK, S5_CW, S5_SW)

    def out_map(c):
        c = c.reshape(depth, S5_NBLK, S5_GB, S5_GROUP, S5_STATE)
        w = jnp.einsum("lbgcn,gh->lbgnhc", c, eye)
        return w.reshape(depth, S5_NBLK, S5_SW, S5_CW)

    bw = jnp.concatenate([in_map(bb_re), in_map(bb_im)], axis=-1).astype(BF16)
    cw = jnp.concatenate([out_map(c_re), -out_map(c_im)], axis=-2).astype(BF16)
    return bw, cw


def _s5_kernel(*refs, nseg, seglen, chained, ntc, nblk):
    if chained:
        (u_ref, bw_ref, cw_ref, d_ref, are_ref, aim_ref,
         y_ref, sre_ref, sim_ref, us_ref, up_ref, x_ref, yp_ref, carry_ref) = refs
    else:
        (u_ref, bw_ref, cw_ref, d_ref, are_ref, aim_ref, h0re_ref, h0im_ref, y_all_ref,
         y_ref, sre_ref, sim_ref, us_ref, up_ref, x_ref, yp_ref) = refs
    sw, cwid = S5_SW, S5_CW
    blocks = range(nblk)

    for k in blocks:
        us_ref[k] = u_ref[:, k * cwid:(k + 1) * cwid]

    def project_in(k, after=None):
        for i in range(seglen):
            rows = us_ref.at[k][pl.ds(i, nseg, stride=seglen), :]
            up_ref[k, i * nseg:(i + 1) * nseg, :] = rows if after is None else rows + after
        x_ref[k] = _dot(up_ref[k].astype(BF16), bw_ref[k])

    def project_out(k):
        y = _dot(x_ref[k].astype(BF16), cw_ref[k]) + d_ref[k] * up_ref[k]
        yp_ref[k] = jax.nn.gelu(y)
        for i in range(seglen):
            us_ref.at[k][pl.ds(i, nseg, stride=seglen), :] = yp_ref[k, i * nseg:(i + 1) * nseg, :]

    def make_step(k, store):
        arb = jnp.broadcast_to(are_ref[k], (SUBLANES, sw))
        aib = jnp.broadcast_to(aim_ref[k], (SUBLANES, sw))

        def step(xr, xi, r0):
            br = x_ref[k, pl.ds(r0, SUBLANES), :sw]
            bi = x_ref[k, pl.ds(r0, SUBLANES), sw:]
            xr, xi = arb * xr - aib * xi + br, arb * xi + aib * xr + bi
            if store:
                x_ref[k, pl.ds(r0, SUBLANES), :sw] = xr
                x_ref[k, pl.ds(r0, SUBLANES), sw:] = xi
            return xr, xi

        return step

    if chained:
        tc = pl.program_id(2)

        @pl.when(tc == 0)
        def _():
            carry_ref[...] = jnp.zeros_like(carry_ref)

        def scan(k, after=None):
            step = make_step(k, False)
            er = ei = jnp.zeros((SUBLANES, sw), F32)
            if after is not None:
                er = er + jnp.tile(after, (1, sw // LANES))
            for i in range(seglen):
                er, ei = step(er, ei, i * SUBLANES)
            pr, pi = are_ref[k], aim_ref[k]
            for _ in range(seglen.bit_length() - 1):
                pr, pi = pr * pr - pi * pi, 2.0 * pr * pi
            cr = carry_ref[k, 0:1, :sw]
            ci = carry_ref[k, 0:1, sw:]
            starts_r, starts_i = [], []
            for j in range(SUBLANES):
                starts_r.append(cr)
                starts_i.append(ci)
                cr, ci = (pr * cr - pi * ci + er[j:j + 1, :],
                          pr * ci + pi * cr + ei[j:j + 1, :])
            carry_ref[k, 0:1, :sw] = cr
            carry_ref[k, 0:1, sw:] = ci
            step = make_step(k, True)
            xr = jnp.concatenate(starts_r, axis=0)
            xi = jnp.concatenate(starts_i, axis=0)
            for i in range(seglen):
                xr, xi = step(xr, xi, i * SUBLANES)
            bits = pltpu.bitcast(xr[:, :LANES], jnp.uint32)
            return pltpu.bitcast(lax.shift_right_logical(bits, jnp.uint32(32)), F32)

        done = {}
        project_in(0)
        project_in(1)
        done[0] = scan(0)
        for t in range(1, nblk - 1):
            project_in(t + 1, done[t - 1])
            done[t] = scan(t, done[t - 1])
            project_out(t - 1)
        scan(nblk - 1, done[nblk - 2])
        project_out(nblk - 2)
        project_out(nblk - 1)

        @pl.when(tc == ntc - 1)
        def _():
            for k in blocks:
                sre_ref[:, k * sw:(k + 1) * sw] = carry_ref[k, 0:1, :sw]
                sim_ref[:, k * sw:(k + 1) * sw] = carry_ref[k, 0:1, sw:]
    else:
        for k in blocks:
            project_in(k)

        def scan_group(sg, _):
            g0 = pl.multiple_of(sg * SUBLANES, SUBLANES)
            for k in blocks:
                step = make_step(k, True)
                xr = h0re_ref[pl.ds(g0, SUBLANES), k * sw:(k + 1) * sw]
                xi = h0im_ref[pl.ds(g0, SUBLANES), k * sw:(k + 1) * sw]
                for i in range(seglen):
                    xr, xi = step(xr, xi, pl.multiple_of(i * nseg + g0, SUBLANES))
                sre_ref[pl.ds(g0, SUBLANES), k * sw:(k + 1) * sw] = xr
                sim_ref[pl.ds(g0, SUBLANES), k * sw:(k + 1) * sw] = xi
            return 0

        lax.fori_loop(0, nseg // SUBLANES, scan_group, 0)
        for k in blocks:
            project_out(k)

    for k in blocks:
        y_ref[:, k * cwid:(k + 1) * cwid] = us_ref[k]


def _s5_prompt(z, bw, cw, d, are, aim, batch, seq):
    nseg, seglen = SUBLANES, 64
    rows = nseg * seglen
    ntc = seq // rows
    gn = S5_GROUPS * S5_STATE
    nsb = S5_NBLK
    wspec = lambda shape: pl.BlockSpec((nsb,) + shape, lambda b, g, t: (g, 0, 0))
    vec = lambda w: wspec((1, w))
    st = pl.BlockSpec((None, 1, nsb * S5_SW), lambda b, g, t: (b, 0, g))
    slab = pltpu.VMEM((nsb, rows, S5_CW), F32)
    y, sre, sim = pl.pallas_call(
        functools.partial(_s5_kernel, nseg=nseg, seglen=seglen, chained=True, ntc=ntc, nblk=nsb),
        grid=(batch, S5_NBLK // nsb, ntc),
        in_specs=[
            pl.BlockSpec((rows, nsb * S5_CW), lambda b, g, t: (b * ntc + t, g)),
            wspec((S5_CW, 2 * S5_SW)),
            wspec((2 * S5_SW, S5_CW)),
            vec(S5_CW), vec(S5_SW), vec(S5_SW),
        ],
        out_specs=[pl.BlockSpec((rows, nsb * S5_CW), lambda b, g, t: (b * ntc + t, g)), st, st],
        out_shape=[jax.ShapeDtypeStruct((z.shape[0], S5_WIDTH), F32),
                   jax.ShapeDtypeStruct((batch, 1, gn), F32),
                   jax.ShapeDtypeStruct((batch, 1, gn), F32)],
        scratch_shapes=[slab, slab,
                        pltpu.VMEM((nsb, rows, 2 * S5_SW), F32),
                        slab,
                        pltpu.VMEM((nsb, SUBLANES, 2 * S5_SW), F32)],
        compiler_params=_params("parallel", "parallel", "arbitrary"),
        name="s5_prompt",
    )(z, bw, cw, d.reshape(S5_NBLK, 1, S5_CW), are.reshape(S5_NBLK, 1, S5_SW),
      aim.reshape(S5_NBLK, 1, S5_SW))
    shape = (batch, S5_GROUPS, S5_STATE)
    return y, sre.reshape(shape), sim.reshape(shape)


def _s5_sample(z, y_all, bw, cw, d, are, aim, h0_re, h0_im, row0, nb, nt):
    rows = nb * nt
    gn = S5_GROUPS * S5_STATE
    rb = row0 // rows
    nsb = S5_STEP_BLOCKS
    wspec = lambda shape: pl.BlockSpec((nsb,) + shape, lambda g: (g, 0, 0))
    vec = lambda w: wspec((1, w))
    st = pl.BlockSpec((nb, nsb * S5_SW), lambda g: (0, g))
    slab = pltpu.VMEM((nsb, rows, S5_CW), F32)
    y, sre, sim = pl.pallas_call(
        functools.partial(_s5_kernel, nseg=nb, seglen=nt, chained=False, ntc=1, nblk=nsb),
        grid=(S5_NBLK // nsb,),
        in_specs=[
            pl.BlockSpec((rows, nsb * S5_CW), lambda g: (rb, g)),
            wspec((S5_CW, 2 * S5_SW)),
            wspec((2 * S5_SW, S5_CW)),
            vec(S5_CW), vec(S5_SW), vec(S5_SW), st, st,
            pl.BlockSpec(memory_space=pl.ANY),
        ],
        out_specs=[pl.BlockSpec((rows, nsb * S5_CW), lambda g: (rb, g)), st, st],
        out_shape=[jax.ShapeDtypeStruct(y_all.shape, F32),
                   jax.ShapeDtypeStruct((nb, gn), F32),
                   jax.ShapeDtypeStruct((nb, gn), F32)],
        input_output_aliases={8: 0},
        scratch_shapes=[slab, slab,
                        pltpu.VMEM((nsb, rows, 2 * S5_SW), F32),
                        slab],
        compiler_params=_params("parallel"),
        name="s5_sample",
    )(z, bw, cw, d.reshape(S5_NBLK, 1, S5_CW), are.reshape(S5_NBLK, 1, S5_SW),
      aim.reshape(S5_NBLK, 1, S5_SW), h0_re.reshape(nb, gn), h0_im.reshape(nb, gn), y_all)
    shape = (nb, S5_GROUPS, S5_STATE)
    return y, sre.reshape(shape), sim.reshape(shape)


def _cumsum_rows(x, seq_len):
    ngroup = GLA_ROWS // SUBLANES
    y = x.reshape(ngroup, SUBLANES, LANES)
    r = lax.broadcasted_iota(jnp.int32, y.shape, 1)
    for d in (1, 2, 4):
        y = y + jnp.where(r >= d, pltpu.roll(y, d, axis=1), 0.0)
    if seq_len == SUBLANES:
        return y.reshape(GLA_ROWS, LANES)
    parts, acc = [], None
    for g in range(ngroup):
        if g % (seq_len // SUBLANES) == 0:
            acc = None
        part = y[g] if acc is None else y[g] + acc
        parts.append(part)
        acc = part[SUBLANES - 1:SUBLANES, :]
    return jnp.concatenate(parts, axis=0)


def _level_ref(b, level):
    half = 1 << level
    span = 2 * half
    if span >= SUBLANES:
        b3 = b.reshape(GLA_ROWS // span, span, LANES)
        return jnp.broadcast_to(b3[:, half - 1:half, :], b3.shape).reshape(GLA_ROWS, LANES)
    b3 = b.reshape(GLA_ROWS // SUBLANES, SUBLANES, LANES)
    r = lax.broadcasted_iota(jnp.int32, b3.shape, 1)
    nspan = SUBLANES // span
    ref = jnp.broadcast_to(b3[:, (nspan - 1) * span + half - 1:(nspan - 1) * span + half, :], b3.shape)
    for p in range(nspan - 2, -1, -1):
        piece = jnp.broadcast_to(b3[:, p * span + half - 1:p * span + half, :], b3.shape)
        ref = jnp.where(r < (p + 1) * span, piece, ref)
    return ref.reshape(GLA_ROWS, LANES)


def _gla_tables(seq_len):
    t = lax.broadcasted_iota(jnp.int32, (GLA_ROWS, GLA_ROWS), 0)
    s = lax.broadcasted_iota(jnp.int32, (GLA_ROWS, GLA_ROWS), 1)
    x = t ^ s
    level = jnp.zeros((GLA_ROWS, GLA_ROWS), jnp.int32)
    for k in range(1, 6):
        level = level + jnp.where(x >= (1 << k), 1, 0)
    level = jnp.where(s < t, level, jnp.where(s == t, GLA_DIAG, -1))
    rows = lax.broadcasted_iota(jnp.int32, (GLA_ROWS, LANES), 0)
    return level, rows


def _gla_block(q, fz, v, lb, states, tables, seq_len, transposed_state):
    level, rows = tables
    nseq = GLA_ROWS // seq_len
    nlev = seq_len.bit_length() - 1
    qc = jax.nn.silu(q)
    fg = lb + (1.0 - lb) * jax.nn.sigmoid(fz)
    kc = 1.0 - fg
    vb = v.astype(BF16)
    b = _cumsum_rows(jnp.log2(fg), seq_len)

    nt = (((1,), (1,)), ((), ()))
    scores = [lax.dot_general(qc.astype(BF16), kc.astype(BF16), nt, preferred_element_type=F32)]
    for lev in range(nlev):
        half = 1 << lev
        if lev == 0:
            upper = (rows & 1) == 1
            qh = jnp.where(upper, qc * fg, 0.0).astype(BF16)
            kh = jnp.where(upper, 0.0, kc).astype(BF16)
        elif half >= SUBLANES:
            shape3 = (GLA_ROWS // (2 * half), 2 * half, LANES)
            b3, q3, k3 = b.reshape(shape3), qc.reshape(shape3), kc.reshape(shape3)
            ref = b3[:, half - 1:half, :]
            zero = jnp.zeros((shape3[0], half, LANES), F32)
            qh = jnp.concatenate([zero, q3[:, half:, :] * jnp.exp2(b3[:, half:, :] - ref)], axis=1)
            kh = jnp.concatenate([k3[:, :half, :] * jnp.exp2(ref - b3[:, :half, :]), zero], axis=1)
            qh = qh.reshape(GLA_ROWS, LANES).astype(BF16)
            kh = kh.reshape(GLA_ROWS, LANES).astype(BF16)
        else:
            w = jnp.exp2(-jnp.abs(b - _level_ref(b, lev)))
            upper = ((rows >> lev) & 1) == 1
            qh = jnp.where(upper, qc * w, 0.0).astype(BF16)
            kh = jnp.where(upper, 0.0, kc * w).astype(BF16)
        scores.append(lax.dot_general(qh, kh, nt, preferred_element_type=F32))
    att_rows = []
    for g in range(GLA_ROWS // SUBLANES):
        rs = slice(g * SUBLANES, (g + 1) * SUBLANES)
        lv = level[rs]
        a = jnp.where(lv == GLA_DIAG, scores[0][rs], 0.0)
        for lev in range(nlev):
            if lev < 3 or (g >> (lev - 3)) & 1:
                a = jnp.where(lv == lev, scores[lev + 1][rs], a)
        att_rows.append(a)
    o = _dot(jnp.concatenate(att_rows, axis=0).astype(BF16), vb)

    b3 = b.reshape(nseq, seq_len, LANES)
    blast = jnp.broadcast_to(b3[:, seq_len - 1:seq_len, :], b3.shape).reshape(GLA_ROWS, LANES)
    qe = (qc * jnp.exp2(b)).astype(BF16)
    kd = (kc * jnp.exp2(blast - b)).astype(BF16)
    tn = (((0,), (0,)), ((), ()))
    o_inter, new_states = [], []
    for n in range(nseq):
        sl = slice(n * seq_len, (n + 1) * seq_len)
        s_n = states[n]
        e_row = jnp.exp2(blast[n * seq_len:n * seq_len + 1, :])
        if transposed_state:
            o_inter.append(lax.dot_general(qe[sl], s_n.astype(BF16), nt, preferred_element_type=F32))
            upd = lax.dot_general(vb[sl], kd[sl], tn, preferred_element_type=F32)
            new_states.append(e_row * s_n + upd)
        else:
            o_inter.append(_dot(qe[sl], s_n.astype(BF16)))
            upd = lax.dot_general(kd[sl], vb[sl], tn, preferred_element_type=F32)
            e_col = jnp.transpose(jnp.broadcast_to(e_row, (HGRN_DK, LANES)))
            new_states.append(e_col * s_n + upd)
    o = o + (o_inter[0] if nseq == 1 else jnp.concatenate(o_inter, axis=0))
    return o, new_states


def _lower_bound(lbp, layer):
    if layer == 0:
        return jnp.zeros((1, lbp.shape[1]), F32)
    e = jnp.exp(lbp - jnp.max(lbp, axis=0, keepdims=True))
    p = e / jnp.sum(e, axis=0, keepdims=True)
    return jnp.sum(p[1:layer + 1, :], axis=0, keepdims=True)


def _gla_finish(o, g, gain):
    o = o * lax.rsqrt(jnp.mean(o * o, axis=-1, keepdims=True) + EPS)
    return o * gain * jax.nn.silu(g)


def _hgrn_prompt_kernel(q_ref, f_ref, v_ref, g_ref, lbp_ref, gain_ref, o_ref, sout_ref, s_ref,
                        *, layer, nchunk, ntb):
    tb = pl.program_id(2)

    @pl.when(tb == 0)
    def _():
        s_ref[...] = jnp.zeros_like(s_ref)

    tables = _gla_tables(GLA_ROWS)
    lb = _lower_bound(lbp_ref[...], layer)
    gain = gain_ref[layer:layer + 1, :]

    def chunk(c, _):
        r = pl.ds(pl.multiple_of(c * GLA_ROWS, GLA_ROWS), GLA_ROWS)
        for hh in range(HGRN_HB):
            cs = slice(hh * LANES, (hh + 1) * LANES)
            o, (s_new,) = _gla_block(q_ref[r, cs], f_ref[r, cs], v_ref[r, cs], lb[:, cs],
                                     [s_ref[hh]], tables, GLA_ROWS, True)
            s_ref[hh] = s_new
            o_ref[r, cs] = _gla_finish(o, g_ref[r, cs], gain[:, cs])
        return 0

    lax.fori_loop(0, nchunk, chunk, 0, unroll=4)

    @pl.when(tb == ntb - 1)
    def _():
        for hh in range(HGRN_HB):
            sout_ref[hh] = jnp.transpose(s_ref[hh])


def _hgrn_prompt(z, lbp, gain, layer, batch, seq):
    tb_rows = 1024 if seq % 1024 == 0 else 512
    ntb = seq // tb_rows
    wb = HGRN_HB * LANES
    nhb = HGRN_HEADS // HGRN_HB

    def zspec(k):
        c0 = (S5_WIDTH + k * HGRN_WIDTH) // wb
        return pl.BlockSpec((tb_rows, wb), lambda b, h, t: (b * ntb + t, c0 + h))

    par = pl.BlockSpec((lbp.shape[0], wb), lambda b, h, t: (0, h))
    return pl.pallas_call(
        functools.partial(_hgrn_prompt_kernel, layer=layer, nchunk=tb_rows // GLA_ROWS, ntb=ntb),
        grid=(batch, nhb, ntb),
        in_specs=[zspec(0), zspec(1), zspec(2), zspec(3), par, par],
        out_specs=[pl.BlockSpec((tb_rows, wb), lambda b, h, t: (b * ntb + t, h)),
                   pl.BlockSpec((None, HGRN_HB, HGRN_DK, HGRN_DV), lambda b, h, t: (b, h, 0, 0))],
        out_shape=[jax.ShapeDtypeStruct((z.shape[0], HGRN_WIDTH), F32),
                   jax.ShapeDtypeStruct((batch, HGRN_HEADS, HGRN_DK, HGRN_DV), F32)],
        scratch_shapes=[pltpu.VMEM((HGRN_HB, HGRN_DK, HGRN_DV), F32)],
        compiler_params=_params("parallel", "parallel", "arbitrary"),
        name="hgrn_prompt",
    )(z, z, z, z, lbp, gain)


def _hgrn_sample_kernel(q_ref, f_ref, v_ref, g_ref, lbp_ref, gain_ref, s0_ref, *rest, layer, nt):
    o_ref, sout_ref = rest[-2:]
    nseq = GLA_ROWS // nt
    tables = _gla_tables(nt)
    lb = _lower_bound(lbp_ref[...], layer)
    gain = gain_ref[layer:layer + 1, :]
    for hh in range(HGRN_HB):
        cs = slice(hh * LANES, (hh + 1) * LANES)
        o, new_states = _gla_block(q_ref[:, cs], f_ref[:, cs], v_ref[:, cs], lb[:, cs],
                                   [s0_ref[n, hh] for n in range(nseq)], tables, nt, False)
        for n in range(nseq):
            sout_ref[n, hh] = new_states[n].astype(sout_ref.dtype)
        o_ref[:, cs] = _gla_finish(o, g_ref[:, cs], gain[:, cs])


def _hgrn_sample(z, y_all, lbp, gain, s0, s_all, layer, row0, nb, nt):
    nseq = GLA_ROWS // nt
    rb0 = row0 // GLA_ROWS
    wb = HGRN_HB * LANES

    def zspec(k):
        c0 = (S5_WIDTH + k * HGRN_WIDTH) // wb
        return pl.BlockSpec((GLA_ROWS, wb), lambda h, i: (rb0 + i, c0 + h))

    par = pl.BlockSpec((lbp.shape[0], wb), lambda h, i: (0, h))
    sspec = pl.BlockSpec((None, nseq, HGRN_HB, HGRN_DK, HGRN_DV), lambda h, i: (layer, i, h, 0, 0))
    anyspec = pl.BlockSpec(memory_space=pl.ANY)
    carried = (y_all,) if s_all is None else (y_all, s_all)
    aliases = {7: 0} if s_all is None else {7: 0, 8: 1}
    return pl.pallas_call(
        functools.partial(_hgrn_sample_kernel, layer=layer, nt=nt),
        grid=(HGRN_HEADS // HGRN_HB, nb // nseq),
        in_specs=[zspec(0), zspec(1), zspec(2), zspec(3), par, par, sspec] + [anyspec] * len(carried),
        out_specs=[pl.BlockSpec((GLA_ROWS, wb), lambda h, i: (rb0 + i, h)), sspec],
        out_shape=[jax.ShapeDtypeStruct(y_all.shape, F32),
                   jax.ShapeDtypeStruct(s0.shape, s0.dtype)],
        input_output_aliases=aliases,
        compiler_params=_params("parallel", "parallel"),
        name="hgrn_sample",
    )(z, z, z, z, lbp, gain, s0, *carried)


def _mixout_kernel(h_ref, ys_ref, yh_ref, wglu_ref, bglu_ref, wo_ref, gpost_ref, o_ref):
    ys = ys_ref[...]
    gate = jax.nn.sigmoid(_dot(ys.astype(BF16), wglu_ref[...]) + bglu_ref[...])
    out = (_dot((ys * gate).astype(BF16), wo_ref[:S5_WIDTH, :])
           + _dot(yh_ref[...].astype(BF16), wo_ref[S5_WIDTH:, :]))
    o_ref[...] = h_ref[...] + _rms(out, gpost_ref[...])


def _mixout(h, ys, yh, w_glu, b_glu, w_out, gpost):
    m = h.shape[0]
    tm = _row_tile(m)
    full = lambda shape: pl.BlockSpec(shape, lambda i: (0, 0))
    return pl.pallas_call(
        _mixout_kernel,
        grid=(m // tm,),
        in_specs=[
            pl.BlockSpec((tm, D_MODEL), lambda i: (i, 0)),
            pl.BlockSpec((tm, S5_WIDTH), lambda i: (i, 0)),
            pl.BlockSpec((tm, HGRN_WIDTH), lambda i: (i, 0)),
            full((S5_WIDTH, S5_WIDTH)), full((1, S5_WIDTH)),
            full((D_MODEL, D_MODEL)), full((1, D_MODEL)),
        ],
        out_specs=pl.BlockSpec((tm, D_MODEL), lambda i: (i, 0)),
        out_shape=jax.ShapeDtypeStruct((m, D_MODEL), F32),
        compiler_params=_params("parallel"),
        name="mixout",
    )(h, ys, yh, w_glu, b_glu, w_out, gpost)


def kernel(x_prompt, x_sample, state_s5_re, state_s5_im, state_hgrn, norm_pre, norm_post, ffn1_w_gate, ffn1_w_up, ffn1_w_down, ffn2_w_gate, ffn2_w_up, ffn2_w_down, w_in, w_out, s5_lam_re, s5_lam_im, s5_log_dt, s5_b_re, s5_b_im, s5_c_re, s5_c_im, s5_d, s5_w_glu, s5_b_glu, hgrn_lb, hgrn_norm):
    batch, seq, _ = x_prompt.shape
    nb, nt, _ = x_sample.shape
    depth = norm_pre.shape[0]
    mp = batch * seq
    sdt = state_hgrn.dtype

    h = jnp.concatenate([x_prompt.reshape(mp, D_MODEL), x_sample.reshape(nb * nt, D_MODEL)], axis=0)

    are, aim, bb_re, bb_im = _s5_discretise(s5_lam_re, s5_lam_im, s5_log_dt, s5_b_re, s5_b_im)
    bw, cw = _s5_block_weights(bb_re, bb_im, s5_c_re, s5_c_im)

    vec = lambda a: a.reshape(1, -1)

    small = ("re_p", "im_p", "h_p", "re_s", "im_s")
    outs = {k: [] for k in small}
    h_s_all = None
    for l in range(depth):
        h = _ffn(h, vec(norm_pre[l, 0]), vec(norm_post[l, 0]),
                 ffn1_w_gate, ffn1_w_up, ffn1_w_down, l)

        z = _mixin(h, vec(norm_pre[l, 1]), w_in, l)
        d = vec(s5_d[l])
        ys, re_p, im_p = _s5_prompt(z, bw[l], cw[l], d, are[l], aim[l], batch, seq)
        ys, re_s, im_s = _s5_sample(z, ys, bw[l], cw[l], d, are[l], aim[l],
                                    state_s5_re[l], state_s5_im[l], mp, nb, nt)
        yh, h_p = _hgrn_prompt(z, hgrn_lb, hgrn_norm, l, batch, seq)
        yh, h_s_all = _hgrn_sample(z, yh, hgrn_lb, hgrn_norm, state_hgrn, h_s_all, l, mp, nb, nt)
        h = _mixout(h, ys, yh, s5_w_glu[l].astype(BF16), vec(s5_b_glu[l]), w_out[l].astype(BF16),
                    vec(norm_post[l, 1]))

        h = _ffn(h, vec(norm_pre[l, 2]), vec(norm_post[l, 2]),
                 ffn2_w_gate, ffn2_w_up, ffn2_w_down, l)
        for k, a in zip(small, (re_p, im_p, h_p, re_s, im_s)):
            outs[k].append(a)

    stack = lambda k: jnp.stack(outs[k]).astype(sdt)
    return (h[:mp].reshape(batch, seq, D_MODEL), h[mp:].reshape(nb, nt, D_MODEL),
            stack("re_p"), stack("im_p"), stack("h_p"),
            stack("re_s"), stack("im_s"), h_s_all)
```

```python
import functools

import jax
import jax.numpy as jnp
from jax import lax
from jax.experimental import pallas as pl
from jax.experimental.pallas import tpu as pltpu

F32 = jnp.float32
BF16 = jnp.bfloat16

D_MODEL = 2048
S5_WIDTH = 1024
S5_GROUP = 16
S5_GROUPS = 64
S5_STATE = 64
HGRN_WIDTH = 1024
HGRN_DK = 128
HGRN_DV = 128
HGRN_HEADS = 8
D_FF = 5504
IN_WIDTH = S5_WIDTH + 4 * HGRN_WIDTH
EPS = 1e-6

LANES = 128
SUBLANES = 8
VMEM_LIMIT_BYTES = 60 * 1024 * 1024

TOKEN_TILE = 1024
FF_TILE = 256
MIXIN_ROWS = 1536
IN_TILE = 512
NORM_ROWS = 256
HGRN_HB = 8
S5_GB = 8
S5_NBLK = S5_GROUPS // S5_GB
S5_STEP_BLOCKS = 4
S5_SW = S5_GB * S5_STATE
S5_CW = S5_GB * S5_GROUP
GLA_ROWS = 64
GLA_DIAG = 7


def _params(*sem):
    return pltpu.CompilerParams(dimension_semantics=sem, vmem_limit_bytes=VMEM_LIMIT_BYTES)


def _row_tile(m, largest=512):
    for t in (1024, 512, 256, 128, 64):
        if t <= largest and m % t == 0:
            return t
    raise ValueError(f"token count {m} must be a multiple of 64")


def _rms(x, gain):
    return x * lax.rsqrt(jnp.mean(x * x, axis=-1, keepdims=True) + EPS) * gain


def _dot(a, b):
    return jnp.dot(a, b, preferred_element_type=F32)


def _for_row_chunks(nrows, fn, rows_per_slice=NORM_ROWS):
    step = min(rows_per_slice, nrows)
    for c in range(nrows // step):
        fn(pl.ds(c * step, step))


def _ffn_kernel(x_ref, gpre_ref, gpost_ref, wg_ref, wu_ref, wd_ref, o_ref, xn_ref, *, nj):
    j = pl.program_id(1)
    nrows = x_ref.shape[0]
    nvalid_last = D_FF - (nj - 1) * FF_TILE

    def weights(last):
        wd = wd_ref[...]
        if last:
            row = lax.broadcasted_iota(jnp.int32, wd.shape, 0)
            wd = jnp.where(row < nvalid_last, wd, 0.0)
        return wg_ref[...].astype(BF16), wu_ref[...].astype(BF16), wd.astype(BF16)

    def swiglu(r, w, last):
        xn = xn_ref[r, :]
        a = jax.nn.silu(_dot(xn, w[0])) * _dot(xn, w[1])
        if last:
            col = lax.broadcasted_iota(jnp.int32, a.shape, 1)
            a = jnp.where(col < nvalid_last, a, 0.0)
        return _dot(a.astype(BF16), w[2])

    @pl.when(j == 0)
    def _():
        w = weights(False)

        def first(r):
            xn_ref[r, :] = _rms(x_ref[r, :], gpre_ref[...]).astype(BF16)
            o_ref[r, :] = swiglu(r, w, False)

        _for_row_chunks(nrows, first)

    @pl.when((j > 0) & (j < nj - 1))
    def _():
        w = weights(False)

        def middle(r):
            o_ref[r, :] += swiglu(r, w, False)

        _for_row_chunks(nrows, middle, nrows)

    @pl.when(j == nj - 1)
    def _():
        w = weights(True)

        def last(r):
            acc = o_ref[r, :] + swiglu(r, w, True)
            o_ref[r, :] = x_ref[r, :] + 0.5 * _rms(acc, gpost_ref[...])

        _for_row_chunks(nrows, last)


def _ffn(h, gpre, gpost, wg, wu, wd, layer):
    m = h.shape[0]
    tm = _row_tile(m, TOKEN_TILE)
    nj = pl.cdiv(D_FF, FF_TILE)
    assert nj >= 2
    return pl.pallas_call(
        functools.partial(_ffn_kernel, nj=nj),
        grid=(m // tm, nj),
        in_specs=[
            pl.BlockSpec((tm, D_MODEL), lambda i, j: (i, 0)),
            pl.BlockSpec((1, D_MODEL), lambda i, j: (0, 0)),
            pl.BlockSpec((1, D_MODEL), lambda i, j: (0, 0)),
            pl.BlockSpec((None, D_MODEL, FF_TILE), lambda i, j: (layer, 0, j)),
            pl.BlockSpec((None, D_MODEL, FF_TILE), lambda i, j: (layer, 0, j)),
            pl.BlockSpec((None, FF_TILE, D_MODEL), lambda i, j: (layer, j, 0)),
        ],
        out_specs=pl.BlockSpec((tm, D_MODEL), lambda i, j: (i, 0)),
        out_shape=jax.ShapeDtypeStruct((m, D_MODEL), F32),
        scratch_shapes=[pltpu.VMEM((tm, D_MODEL), BF16)],
        compiler_params=_params("parallel", "arbitrary"),
        name="ffn",
    )(h, gpre, gpost, wg, wu, wd)


def _mixin_kernel(x_ref, gpre_ref, w_ref, o_ref, xn_ref):
    j = pl.program_id(1)

    @pl.when(j == 0)
    def _():
        w = w_ref[...].astype(BF16)

        def first(r):
            xn_ref[r, :] = _rms(x_ref[r, :], gpre_ref[...]).astype(BF16)
            o_ref[r, :] = _dot(xn_ref[r, :], w)

        _for_row_chunks(x_ref.shape[0], first)

    @pl.when(j > 0)
    def _():
        o_ref[...] = _dot(xn_ref[...], w_ref[...].astype(BF16))


def _mixin(h, gpre, w_in, layer):
    m = h.shape[0]
    tm = MIXIN_ROWS if m % MIXIN_ROWS == 0 else _row_tile(m, TOKEN_TILE)
    return pl.pallas_call(
        _mixin_kernel,
        grid=(m // tm, IN_WIDTH // IN_TILE),
        in_specs=[
            pl.BlockSpec((tm, D_MODEL), lambda i, j: (i, 0)),
            pl.BlockSpec((1, D_MODEL), lambda i, j: (0, 0)),
            pl.BlockSpec((None, D_MODEL, IN_TILE), lambda i, j: (layer, 0, j)),
        ],
        out_specs=pl.BlockSpec((tm, IN_TILE), lambda i, j: (i, j)),
        out_shape=jax.ShapeDtypeStruct((m, IN_WIDTH), F32),
        scratch_shapes=[pltpu.VMEM((tm, D_MODEL), BF16)],
        compiler_params=_params("parallel", "arbitrary"),
        name="mixin",
    )(h, gpre, w_in)


def _s5_disc_kernel(lre_ref, lim_ref, ldt_ref, bre_ref, bim_ref,
                    are_ref, aim_ref, bbre_ref, bbim_ref):
    lam_re = lre_ref[...]
    lam_im = lim_ref[...]
    dt = jnp.exp(ldt_ref[...])
    mag = jnp.exp(lam_re * dt)
    ang = lam_im * dt
    abar_re = mag * jnp.cos(ang)
    abar_im = mag * jnp.sin(ang)
    p = abar_re - 1.0
    den = lam_re * lam_re + lam_im * lam_im
    z_re = (p * lam_re + abar_im * lam_im) / den
    z_im = (abar_im * lam_re - p * lam_im) / den
    are_ref[...] = abar_re
    aim_ref[...] = abar_im
    b_re = bre_ref[...]
    b_im = bim_ref[...]
    bbre_ref[...] = z_re * b_re - z_im * b_im
    bbim_ref[...] = z_re * b_im + z_im * b_re


def _s5_discretise(lam_re, lam_im, log_dt, b_re, b_im):
    depth = lam_re.shape[0]
    gn = S5_GROUPS * S5_STATE
    flat = lambda a: a.reshape(depth, 1, gn)
    ldt = jnp.broadcast_to(log_dt[:, :, None], lam_re.shape)
    chan_major = lambda b: jnp.transpose(b, (0, 3, 1, 2)).reshape(depth, S5_GROUP, gn)
    row = pl.BlockSpec((None, 1, gn), lambda l: (l, 0, 0))
    mat = pl.BlockSpec((None, S5_GROUP, gn), lambda l: (l, 0, 0))
    return pl.pallas_call(
        _s5_disc_kernel,
        grid=(depth,),
        in_specs=[row, row, row, mat, mat],
        out_specs=[row, row, mat, mat],
        out_shape=[jax.ShapeDtypeStruct((depth, 1, gn), F32)] * 2
        + [jax.ShapeDtypeStruct((depth, S5_GROUP, gn), F32)] * 2,
        compiler_params=_params("parallel"),
        name="s5_discretise",
    )(flat(lam_re), flat(lam_im), flat(ldt), chan_major(b_re), chan_major(b_im))


def _s5_block_weights(bb_re, bb_im, c_re, c_im):
    depth = bb_re.shape[0]
    eye = jnp.eye(S5_GB, dtype=F32)

    def in_map(bb):
        bb = bb.reshape(depth, S5_GROUP, S5_NBLK, S5_GB, S5_STATE)
        w = jnp.einsum("lcbgn,gh->lbgchn", bb, eye)
        return w.reshape(depth, S5_NBLK, S5_CW, S5_SW)

    def out_map(c):
        c = c.reshape(depth, S5_NBLK, S5_GB, S5_GROUP, S5_STATE)
        w = jnp.einsum("lbgcn,gh->lbgnhc", c, eye)
        return w.reshape(depth, S5_NBLK, S5_SW, S5_CW)

    bw = jnp.concatenate([in_map(bb_re), in_map(bb_im)], axis=-1).astype(BF16)
    cw = jnp.concatenate([out_map(c_re), -out_map(c_im)], axis=-2).astype(BF16)
    return bw, cw


def _s5_kernel(*refs, nseg, seglen, chained, ntc, nblk):
    if chained:
        (u_ref, bw_ref, cw_ref, d_ref, are_ref, aim_ref,
         y_ref, sre_ref, sim_ref, us_ref, up_ref, x_ref, yp_ref, carry_ref) = refs
    else:
        (u_ref, bw_ref, cw_ref, d_ref, are_ref, aim_ref, h0re_ref, h0im_ref, y_all_ref,
         y_ref, sre_ref, sim_ref, us_ref, up_ref, x_ref, yp_ref) = refs
    sw, cwid = S5_SW, S5_CW
    blocks = range(nblk)

    for k in blocks:
        us_ref[k] = u_ref[:, k * cwid:(k + 1) * cwid]

    def project_in(k, after=None):
        for i in range(seglen):
            rows = us_ref.at[k][pl.ds(i, nseg, stride=seglen), :]
            up_ref[k, i * nseg:(i + 1) * nseg, :] = rows if after is None else rows + after
        x_ref[k] = _dot(up_ref[k].astype(BF16), bw_ref[k])

    def project_out(k):
        y = _dot(x_ref[k].astype(BF16), cw_ref[k]) + d_ref[k] * up_ref[k]
        yp_ref[k] = jax.nn.gelu(y)
        for i in range(seglen):
            us_ref.at[k][pl.ds(i, nseg, stride=seglen), :] = yp_ref[k, i * nseg:(i + 1) * nseg, :]

    def make_step(k, store):
        arb = jnp.broadcast_to(are_ref[k], (SUBLANES, sw))
        aib = jnp.broadcast_to(aim_ref[k], (SUBLANES, sw))

        def step(xr, xi, r0):
            br = x_ref[k, pl.ds(r0, SUBLANES), :sw]
            bi = x_ref[k, pl.ds(r0, SUBLANES), sw:]
            xr, xi = arb * xr - aib * xi + br, arb * xi + aib * xr + bi
            if store:
                x_ref[k, pl.ds(r0, SUBLANES), :sw] = xr
                x_ref[k, pl.ds(r0, SUBLANES), sw:] = xi
            return xr, xi

        return step

    if chained:
        tc = pl.program_id(2)

        @pl.when(tc == 0)
        def _():
            carry_ref[...] = jnp.zeros_like(carry_ref)

        def scan(k, after=None):
            step = make_step(k, False)
            er = ei = jnp.zeros((SUBLANES, sw), F32)
            if after is not None:
                er = er + jnp.tile(after, (1, sw // LANES))
            for i in range(seglen):
                er, ei = step(er, ei, i * SUBLANES)
            pr, pi = are_ref[k], aim_ref[k]
            for _ in range(seglen.bit_length() - 1):
                pr, pi = pr * pr - pi * pi, 2.0 * pr * pi
            cr = carry_ref[k, 0:1, :sw]
            ci = carry_ref[k, 0:1, sw:]
            starts_r, starts_i = [], []
            for j in range(SUBLANES):
                starts_r.append(cr)
                starts_i.append(ci)
                cr, ci = (pr * cr - pi * ci + er[j:j + 1, :],
                          pr * ci + pi * cr + ei[j:j + 1, :])
            carry_ref[k, 0:1, :sw] = cr
            carry_ref[k, 0:1, sw:] = ci
            step = make_step(k, True)
            xr = jnp.concatenate(starts_r, axis=0)
            xi = jnp.concatenate(starts_i, axis=0)
            for i in range(seglen):
                xr, xi = step(xr, xi, i * SUBLANES)
            bits = pltpu.bitcast(xr[:, :LANES], jnp.uint32)
            return pltpu.bitcast(lax.shift_right_logical(bits, jnp.uint32(32)), F32)

        done = {}
        project_in(0)
        project_in(1)
        done[0] = scan(0)
        for t in range(1, nblk - 1):
            project_in(t + 1, done[t - 1])
            done[t] = scan(t, done[t - 1])
            project_out(t - 1)
        scan(nblk - 1, done[nblk - 2])
        project_out(nblk - 2)
        project_out(nblk - 1)

        @pl.when(tc == ntc - 1)
        def _():
            for k in blocks:
                sre_ref[:, k * sw:(k + 1) * sw] = carry_ref[k, 0:1, :sw]
                sim_ref[:, k * sw:(k + 1) * sw] = carry_ref[k, 0:1, sw:]
    else:
        for k in blocks:
            project_in(k)

        def scan_group(sg, _):
            g0 = pl.multiple_of(sg * SUBLANES, SUBLANES)
            for k in blocks:
                step = make_step(k, True)
                xr = h0re_ref[pl.ds(g0, SUBLANES), k * sw:(k + 1) * sw]
                xi = h0im_ref[pl.ds(g0, SUBLANES), k * sw:(k + 1) * sw]
                for i in range(seglen):
                    xr, xi = step(xr, xi, pl.multiple_of(i * nseg + g0, SUBLANES))
                sre_ref[pl.ds(g0, SUBLANES), k * sw:(k + 1) * sw] = xr
                sim_ref[pl.ds(g0, SUBLANES), k * sw:(k + 1) * sw] = xi
            return 0

        lax.fori_loop(0, nseg // SUBLANES, scan_group, 0)
        for k in blocks:
            project_out(k)

    for k in blocks:
        y_ref[:, k * cwid:(k + 1) * cwid] = us_ref[k]


def _s5_prompt(z, bw, cw, d, are, aim, batch, seq):
    nseg, seglen = SUBLANES, 64
    rows = nseg * seglen
    ntc = seq // rows
    gn = S5_GROUPS * S5_STATE
    nsb = S5_NBLK
    wspec = lambda shape: pl.BlockSpec((nsb,) + shape, lambda b, g, t: (g, 0, 0))
    vec = lambda w: wspec((1, w))
    st = pl.BlockSpec((None, 1, nsb * S5_SW), lambda b, g, t: (b, 0, g))
    slab = pltpu.VMEM((nsb, rows, S5_CW), F32)
    y, sre, sim = pl.pallas_call(
        functools.partial(_s5_kernel, nseg=nseg, seglen=seglen, chained=True, ntc=ntc, nblk=nsb),
        grid=(batch, S5_NBLK // nsb, ntc),
        in_specs=[
            pl.BlockSpec((rows, nsb * S5_CW), lambda b, g, t: (b * ntc + t, g)),
            wspec((S5_CW, 2 * S5_SW)),
            wspec((2 * S5_SW, S5_CW)),
            vec(S5_CW), vec(S5_SW), vec(S5_SW),
        ],
        out_specs=[pl.BlockSpec((rows, nsb * S5_CW), lambda b, g, t: (b * ntc + t, g)), st, st],
        out_shape=[jax.ShapeDtypeStruct((z.shape[0], S5_WIDTH), F32),
                   jax.ShapeDtypeStruct((batch, 1, gn), F32),
                   jax.ShapeDtypeStruct((batch, 1, gn), F32)],
        scratch_shapes=[slab, slab,
                        pltpu.VMEM((nsb, rows, 2 * S5_SW), F32),
                        slab,
                        pltpu.VMEM((nsb, SUBLANES, 2 * S5_SW), F32)],
        compiler_params=_params("parallel", "parallel", "arbitrary"),
        name="s5_prompt",
    )(z, bw, cw, d.reshape(S5_NBLK, 1, S5_CW), are.reshape(S5_NBLK, 1, S5_SW),
      aim.reshape(S5_NBLK, 1, S5_SW))
    shape = (batch, S5_GROUPS, S5_STATE)
    return y, sre.reshape(shape), sim.reshape(shape)


def _s5_sample(z, y_all, bw, cw, d, are, aim, h0_re, h0_im, row0, nb, nt):
    rows = nb * nt
    gn = S5_GROUPS * S5_STATE
    rb = row0 // rows
    nsb = S5_STEP_BLOCKS
    wspec = lambda shape: pl.BlockSpec((nsb,) + shape, lambda g: (g, 0, 0))
    vec = lambda w: wspec((1, w))
    st = pl.BlockSpec((nb, nsb * S5_SW), lambda g: (0, g))
    slab = pltpu.VMEM((nsb, rows, S5_CW), F32)
    y, sre, sim = pl.pallas_call(
        functools.partial(_s5_kernel, nseg=nb, seglen=nt, chained=False, ntc=1, nblk=nsb),
        grid=(S5_NBLK // nsb,),
        in_specs=[
            pl.BlockSpec((rows, nsb * S5_CW), lambda g: (rb, g)),
            wspec((S5_CW, 2 * S5_SW)),
            wspec((2 * S5_SW, S5_CW)),
            vec(S5_CW), vec(S5_SW), vec(S5_SW), st, st,
            pl.BlockSpec(memory_space=pl.ANY),
        ],
        out_specs=[pl.BlockSpec((rows, nsb * S5_CW), lambda g: (rb, g)), st, st],
        out_shape=[jax.ShapeDtypeStruct(y_all.shape, F32),
                   jax.ShapeDtypeStruct((nb, gn), F32),
                   jax.ShapeDtypeStruct((nb, gn), F32)],
        input_output_aliases={8: 0},
        scratch_shapes=[slab, slab,
                        pltpu.VMEM((nsb, rows, 2 * S5_SW), F32),
                        slab],
        compiler_params=_params("parallel"),
        name="s5_sample",
    )(z, bw, cw, d.reshape(S5_NBLK, 1, S5_CW), are.reshape(S5_NBLK, 1, S5_SW),
      aim.reshape(S5_NBLK, 1, S5_SW), h0_re.reshape(nb, gn), h0_im.reshape(nb, gn), y_all)
    shape = (nb, S5_GROUPS, S5_STATE)
    return y, sre.reshape(shape), sim.reshape(shape)


def _cumsum_rows(x, seq_len):
    ngroup = GLA_ROWS // SUBLANES
    y = x.reshape(ngroup, SUBLANES, LANES)
    r = lax.broadcasted_iota(jnp.int32, y.shape, 1)
    for d in (1, 2, 4):
        y = y + jnp.where(r >= d, pltpu.roll(y, d, axis=1), 0.0)
    if seq_len == SUBLANES:
        return y.reshape(GLA_ROWS, LANES)
    parts, acc = [], None
    for g in range(ngroup):
        if g % (seq_len // SUBLANES) == 0:
            acc = None
        part = y[g] if acc is None else y[g] + acc
        parts.append(part)
        acc = part[SUBLANES - 1:SUBLANES, :]
    return jnp.concatenate(parts, axis=0)


def _level_ref(b, level):
    half = 1 << level
    span = 2 * half
    if span >= SUBLANES:
        b3 = b.reshape(GLA_ROWS // span, span, LANES)
        return jnp.broadcast_to(b3[:, half - 1:half, :], b3.shape).reshape(GLA_ROWS, LANES)
    b3 = b.reshape(GLA_ROWS // SUBLANES, SUBLANES, LANES)
    r = lax.broadcasted_iota(jnp.int32, b3.shape, 1)
    nspan = SUBLANES // span
    ref = jnp.broadcast_to(b3[:, (nspan - 1) * span + half - 1:(nspan - 1) * span + half, :], b3.shape)
    for p in range(nspan - 2, -1, -1):
        piece = jnp.broadcast_to(b3[:, p * span + half - 1:p * span + half, :], b3.shape)
        ref = jnp.where(r < (p + 1) * span, piece, ref)
    return ref.reshape(GLA_ROWS, LANES)


def _gla_tables(seq_len):
    t = lax.broadcasted_iota(jnp.int32, (GLA_ROWS, GLA_ROWS), 0)
    s = lax.broadcasted_iota(jnp.int32, (GLA_ROWS, GLA_ROWS), 1)
    x = t ^ s
    level = jnp.zeros((GLA_ROWS, GLA_ROWS), jnp.int32)
    for k in range(1, 6):
        level = level + jnp.where(x >= (1 << k), 1, 0)
    level = jnp.where(s < t, level, jnp.where(s == t, GLA_DIAG, -1))
    rows = lax.broadcasted_iota(jnp.int32, (GLA_ROWS, LANES), 0)
    return level, rows


def _gla_block(q, fz, v, lb, states, tables, seq_len, transposed_state):
    level, rows = tables
    nseq = GLA_ROWS // seq_len
    nlev = seq_len.bit_length() - 1
    qc = jax.nn.silu(q)
    fg = lb + (1.0 - lb) * jax.nn.sigmoid(fz)
    kc = 1.0 - fg
    vb = v.astype(BF16)
    b = _cumsum_rows(jnp.log2(fg), seq_len)

    nt = (((1,), (1,)), ((), ()))
    scores = [lax.dot_general(qc.astype(BF16), kc.astype(BF16), nt, preferred_element_type=F32)]
    for lev in range(nlev):
        half = 1 << lev
        if lev == 0:
            upper = (rows & 1) == 1
            qh = jnp.where(upper, qc * fg, 0.0).astype(BF16)
            kh = jnp.where(upper, 0.0, kc).astype(BF16)
        elif half >= SUBLANES:
            shape3 = (GLA_ROWS // (2 * half), 2 * half, LANES)
            b3, q3, k3 = b.reshape(shape3), qc.reshape(shape3), kc.reshape(shape3)
            ref = b3[:, half - 1:half, :]
            zero = jnp.zeros((shape3[0], half, LANES), F32)
            qh = jnp.concatenate([zero, q3[:, half:, :] * jnp.exp2(b3[:, half:, :] - ref)], axis=1)
            kh = jnp.concatenate([k3[:, :half, :] * jnp.exp2(ref - b3[:, :half, :]), zero], axis=1)
            qh = qh.reshape(GLA_ROWS, LANES).astype(BF16)
            kh = kh.reshape(GLA_ROWS, LANES).astype(BF16)
        else:
            w = jnp.exp2(-jnp.abs(b - _level_ref(b, lev)))
            upper = ((rows >> lev) & 1) == 1
            qh = jnp.where(upper, qc * w, 0.0).astype(BF16)
            kh = jnp.where(upper, 0.0, kc * w).astype(BF16)
        scores.append(lax.dot_general(qh, kh, nt, preferred_element_type=F32))
    att_rows = []
    for g in range(GLA_ROWS // SUBLANES):
        rs = slice(g * SUBLANES, (g + 1) * SUBLANES)
        lv = level[rs]
        a = jnp.where(lv == GLA_DIAG, scores[0][rs], 0.0)
        for lev in range(nlev):
            if lev < 3 or (g >> (lev - 3)) & 1:
                a = jnp.where(lv == lev, scores[lev + 1][rs], a)
        att_rows.append(a)
    o = _dot(jnp.concatenate(att_rows, axis=0).astype(BF16), vb)

    b3 = b.reshape(nseq, seq_len, LANES)
    blast = jnp.broadcast_to(b3[:, seq_len - 1:seq_len, :], b3.shape).reshape(GLA_ROWS, LANES)
    qe = (qc * jnp.exp2(b)).astype(BF16)
    kd = (kc * jnp.exp2(blast - b)).astype(BF16)
    tn = (((0,), (0,)), ((), ()))
    o_inter, new_states = [], []
    for n in range(nseq):
        sl = slice(n * seq_len, (n + 1) * seq_len)
        s_n = states[n]
        e_row = jnp.exp2(blast[n * seq_len:n * seq_len + 1, :])
        if transposed_state:
            o_inter.append(lax.dot_general(qe[sl], s_n.astype(BF16), nt, preferred_element_type=F32))
            upd = lax.dot_general(vb[sl], kd[sl], tn, preferred_element_type=F32)
            new_states.append(e_row * s_n + upd)
        else:
            o_inter.append(_dot(qe[sl], s_n.astype(BF16)))
            upd = lax.dot_general(kd[sl], vb[sl], tn, preferred_element_type=F32)
            e_col = jnp.transpose(jnp.broadcast_to(e_row, (HGRN_DK, LANES)))
            new_states.append(e_col * s_n + upd)
    o = o + (o_inter[0] if nseq == 1 else jnp.concatenate(o_inter, axis=0))
    return o, new_states


def _lower_bound(lbp, layer):
    if layer == 0:
        return jnp.zeros((1, lbp.shape[1]), F32)
    e = jnp.exp(lbp - jnp.max(lbp, axis=0, keepdims=True))
    p = e / jnp.sum(e, axis=0, keepdims=True)
    return jnp.sum(p[1:layer + 1, :], axis=0, keepdims=True)


def _gla_finish(o, g, gain):
    o = o * lax.rsqrt(jnp.mean(o * o, axis=-1, keepdims=True) + EPS)
    return o * gain * jax.nn.silu(g)


def _hgrn_prompt_kernel(q_ref, f_ref, v_ref, g_ref, lbp_ref, gain_ref, o_ref, sout_ref, s_ref,
                        *, layer, nchunk, ntb):
    tb = pl.program_id(2)

    @pl.when(tb == 0)
    def _():
        s_ref[...] = jnp.zeros_like(s_ref)

    tables = _gla_tables(GLA_ROWS)
    lb = _lower_bound(lbp_ref[...], layer)
    gain = gain_ref[layer:layer + 1, :]

    def chunk(c, _):
        r = pl.ds(pl.multiple_of(c * GLA_ROWS, GLA_ROWS), GLA_ROWS)
        for hh in range(HGRN_HB):
            cs = slice(hh * LANES, (hh + 1) * LANES)
            o, (s_new,) = _gla_block(q_ref[r, cs], f_ref[r, cs], v_ref[r, cs], lb[:, cs],
                                     [s_ref[hh]], tables, GLA_ROWS, True)
            s_ref[hh] = s_new
            o_ref[r, cs] = _gla_finish(o, g_ref[r, cs], gain[:, cs])
        return 0

    lax.fori_loop(0, nchunk, chunk, 0, unroll=4)

    @pl.when(tb == ntb - 1)
    def _():
        for hh in range(HGRN_HB):
            sout_ref[hh] = jnp.transpose(s_ref[hh])


def _hgrn_prompt(z, lbp, gain, layer, batch, seq):
    tb_rows = 1024 if seq % 1024 == 0 else 512
    ntb = seq // tb_rows
    wb = HGRN_HB * LANES
    nhb = HGRN_HEADS // HGRN_HB

    def zspec(k):
        c0 = (S5_WIDTH + k * HGRN_WIDTH) // wb
        return pl.BlockSpec((tb_rows, wb), lambda b, h, t: (b * ntb + t, c0 + h))

    par = pl.BlockSpec((lbp.shape[0], wb), lambda b, h, t: (0, h))
    return pl.pallas_call(
        functools.partial(_hgrn_prompt_kernel, layer=layer, nchunk=tb_rows // GLA_ROWS, ntb=ntb),
        grid=(batch, nhb, ntb),
        in_specs=[zspec(0), zspec(1), zspec(2), zspec(3), par, par],
        out_specs=[pl.BlockSpec((tb_rows, wb), lambda b, h, t: (b * ntb + t, h)),
                   pl.BlockSpec((None, HGRN_HB, HGRN_DK, HGRN_DV), lambda b, h, t: (b, h, 0, 0))],
        out_shape=[jax.ShapeDtypeStruct((z.shape[0], HGRN_WIDTH), F32),
                   jax.ShapeDtypeStruct((batch, HGRN_HEADS, HGRN_DK, HGRN_DV), F32)],
        scratch_shapes=[pltpu.VMEM((HGRN_HB, HGRN_DK, HGRN_DV), F32)],
        compiler_params=_params("parallel", "parallel", "arbitrary"),
        name="hgrn_prompt",
    )(z, z, z, z, lbp, gain)


def _hgrn_sample_kernel(q_ref, f_ref, v_ref, g_ref, lbp_ref, gain_ref, s0_ref, *rest, layer, nt):
    o_ref, sout_ref = rest[-2:]
    nseq = GLA_ROWS // nt
    tables = _gla_tables(nt)
    lb = _lower_bound(lbp_ref[...], layer)
    gain = gain_ref[layer:layer + 1, :]
    for hh in range(HGRN_HB):
        cs = slice(hh * LANES, (hh + 1) * LANES)
        o, new_states = _gla_block(q_ref[:, cs], f_ref[:, cs], v_ref[:, cs], lb[:, cs],
                                   [s0_ref[n, hh] for n in range(nseq)], tables, nt, False)
        for n in range(nseq):
            sout_ref[n, hh] = new_states[n].astype(sout_ref.dtype)
        o_ref[:, cs] = _gla_finish(o, g_ref[:, cs], gain[:, cs])


def _hgrn_sample(z, y_all, lbp, gain, s0, s_all, layer, row0, nb, nt):
    nseq = GLA_ROWS // nt
    rb0 = row0 // GLA_ROWS
    wb = HGRN_HB * LANES

    def zspec(k):
        c0 = (S5_WIDTH + k * HGRN_WIDTH) // wb
        return pl.BlockSpec((GLA_ROWS, wb), lambda h, i: (rb0 + i, c0 + h))

    par = pl.BlockSpec((lbp.shape[0], wb), lambda h, i: (0, h))
    sspec = pl.BlockSpec((None, nseq, HGRN_HB, HGRN_DK, HGRN_DV), lambda h, i: (layer, i, h, 0, 0))
    anyspec = pl.BlockSpec(memory_space=pl.ANY)
    carried = (y_all,) if s_all is None else (y_all, s_all)
    aliases = {7: 0} if s_all is None else {7: 0, 8: 1}
    return pl.pallas_call(
        functools.partial(_hgrn_sample_kernel, layer=layer, nt=nt),
        grid=(HGRN_HEADS // HGRN_HB, nb // nseq),
        in_specs=[zspec(0), zspec(1), zspec(2), zspec(3), par, par, sspec] + [anyspec] * len(carried),
        out_specs=[pl.BlockSpec((GLA_ROWS, wb), lambda h, i: (rb0 + i, h)), sspec],
        out_shape=[jax.ShapeDtypeStruct(y_all.shape, F32),
                   jax.ShapeDtypeStruct(s0.shape, s0.dtype)],
        input_output_aliases=aliases,
        compiler_params=_params("parallel", "parallel"),
        name="hgrn_sample",
    )(z, z, z, z, lbp, gain, s0, *carried)


def _mixout_kernel(h_ref, ys_ref, yh_ref, wglu_ref, bglu_ref, wo_ref, gpost_ref, o_ref):
    def rows(r):
        ys = ys_ref[r, :]
        gate = jax.nn.sigmoid(_dot(ys.astype(BF16), wglu_ref[...]) + bglu_ref[...])
        out = (_dot((ys * gate).astype(BF16), wo_ref[:S5_WIDTH, :])
               + _dot(yh_ref[r, :].astype(BF16), wo_ref[S5_WIDTH:, :]))
        o_ref[r, :] = h_ref[r, :] + _rms(out, gpost_ref[...])

    _for_row_chunks(h_ref.shape[0], rows)


def _mixout(h, ys, yh, w_glu, b_glu, w_out, gpost):
    m = h.shape[0]
    tm = _row_tile(m)
    full = lambda shape: pl.BlockSpec(shape, lambda i: (0, 0))
    return pl.pallas_call(
        _mixout_kernel,
        grid=(m // tm,),
        in_specs=[
            pl.BlockSpec((tm, D_MODEL), lambda i: (i, 0)),
            pl.BlockSpec((tm, S5_WIDTH), lambda i: (i, 0)),
            pl.BlockSpec((tm, HGRN_WIDTH), lambda i: (i, 0)),
            full((S5_WIDTH, S5_WIDTH)), full((1, S5_WIDTH)),
            full((D_MODEL, D_MODEL)), full((1, D_MODEL)),
        ],
        out_specs=pl.BlockSpec((tm, D_MODEL), lambda i: (i, 0)),
        out_shape=jax.ShapeDtypeStruct((m, D_MODEL), F32),
        compiler_params=_params("parallel"),
        name="mixout",
    )(h, ys, yh, w_glu, b_glu, w_out, gpost)


def kernel(x_prompt, x_sample, state_s5_re, state_s5_im, state_hgrn, norm_pre, norm_post, ffn1_w_gate, ffn1_w_up, ffn1_w_down, ffn2_w_gate, ffn2_w_up, ffn2_w_down, w_in, w_out, s5_lam_re, s5_lam_im, s5_log_dt, s5_b_re, s5_b_im, s5_c_re, s5_c_im, s5_d, s5_w_glu, s5_b_glu, hgrn_lb, hgrn_norm):
    batch, seq, _ = x_prompt.shape
    nb, nt, _ = x_sample.shape
    depth = norm_pre.shape[0]
    mp = batch * seq
    sdt = state_hgrn.dtype

    h = jnp.concatenate([x_prompt.reshape(mp, D_MODEL), x_sample.reshape(nb * nt, D_MODEL)], axis=0)

    are, aim, bb_re, bb_im = _s5_discretise(s5_lam_re, s5_lam_im, s5_log_dt, s5_b_re, s5_b_im)
    bw, cw = _s5_block_weights(bb_re, bb_im, s5_c_re, s5_c_im)

    vec = lambda a: a.reshape(1, -1)

    small = ("re_p", "im_p", "h_p", "re_s", "im_s")
    outs = {k: [] for k in small}
    h_s_all = None
    for l in range(depth):
        h = _ffn(h, vec(norm_pre[l, 0]), vec(norm_post[l, 0]),
                 ffn1_w_gate, ffn1_w_up, ffn1_w_down, l)

        z = _mixin(h, vec(norm_pre[l, 1]), w_in, l)
        d = vec(s5_d[l])
        ys, re_p, im_p = _s5_prompt(z, bw[l], cw[l], d, are[l], aim[l], batch, seq)
        ys, re_s, im_s = _s5_sample(z, ys, bw[l], cw[l], d, are[l], aim[l],
                                    state_s5_re[l], state_s5_im[l], mp, nb, nt)
        yh, h_p = _hgrn_prompt(z, hgrn_lb, hgrn_norm, l, batch, seq)
        yh, h_s_all = _hgrn_sample(z, yh, hgrn_lb, hgrn_norm, state_hgrn, h_s_all, l, mp, nb, nt)
        h = _mixout(h, ys, yh, s5_w_glu[l].astype(BF16), vec(s5_b_glu[l]), w_out[l].astype(BF16),
                    vec(norm_post[l, 1]))

        h = _ffn(h, vec(norm_pre[l, 2]), vec(norm_post[l, 2]),
                 ffn2_w_gate, ffn2_w_up, ffn2_w_down, l)
        for k, a in zip(small, (re_p, im_p, h_p, re_s, im_s)):
            outs[k].append(a)

    stack = lambda k: jnp.stack(outs[k]).astype(sdt)
    return (h[:mp].reshape(batch, seq, D_MODEL), h[mp:].reshape(nb, nt, D_MODEL),
            stack("re_p"), stack("im_p"), stack("h_p"),
            stack("re_s"), stack("im_s"), h_s_all)
```

```python
import functools

import jax
import jax.numpy as jnp
from jax import lax
from jax.experimental import pallas as pl
from jax.experimental.pallas import tpu as pltpu

F32 = jnp.float32
BF16 = jnp.bfloat16

D_MODEL = 2048
S5_WIDTH = 1024
S5_GROUP = 16
S5_GROUPS = 64
S5_STATE = 64
HGRN_WIDTH = 1024
HGRN_DK = 128
HGRN_DV = 128
HGRN_HEADS = 8
D_FF = 5504
IN_WIDTH = S5_WIDTH + 4 * HGRN_WIDTH
EPS = 1e-6

LANES = 128
SUBLANES = 8
VMEM_LIMIT_BYTES = 60 * 1024 * 1024

TOKEN_TILE = 1024
FF_TILE = 256
MIXIN_ROWS = 1536
IN_TILE = 512
NORM_ROWS = 256
HGRN_HB = 8
S5_GB = 8
S5_NBLK = S5_GROUPS // S5_GB
S5_STEP_BLOCKS = 4
S5_SW = S5_GB * S5_STATE
S5_CW = S5_GB * S5_GROUP
GLA_ROWS = 64
GLA_DIAG = 7


def _params(*sem):
    return pltpu.CompilerParams(dimension_semantics=sem, vmem_limit_bytes=VMEM_LIMIT_BYTES)


def _row_tile(m, largest=512):
    for t in (1024, 512, 256, 128, 64):
        if t <= largest and m % t == 0:
            return t
    raise ValueError(f"token count {m} must be a multiple of 64")


def _rms(x, gain):
    return x * lax.rsqrt(jnp.mean(x * x, axis=-1, keepdims=True) + EPS) * gain


def _dot(a, b):
    return jnp.dot(a, b, preferred_element_type=F32)


def _for_row_chunks(nrows, fn, rows_per_slice=NORM_ROWS):
    step = min(rows_per_slice, nrows)
    for c in range(nrows // step):
        fn(pl.ds(c * step, step))


def _ffn_kernel(x_ref, gpre_ref, gpost_ref, wg_ref, wu_ref, wd_ref, o_ref, xn_ref, *, nj):
    j = pl.program_id(1)
    nrows = x_ref.shape[0]
    nvalid_last = D_FF - (nj - 1) * FF_TILE

    def weights(last):
        wd = wd_ref[...]
        if last:
            row = lax.broadcasted_iota(jnp.int32, wd.shape, 0)
            wd = jnp.where(row < nvalid_last, wd, 0.0)
        return wg_ref[...].astype(BF16), wu_ref[...].astype(BF16), wd.astype(BF16)

    def swiglu(r, w, last):
        xn = xn_ref[r, :]
        a = jax.nn.silu(_dot(xn, w[0])) * _dot(xn, w[1])
        if last:
            col = lax.broadcasted_iota(jnp.int32, a.shape, 1)
            a = jnp.where(col < nvalid_last, a, 0.0)
        return _dot(a.astype(BF16), w[2])

    @pl.when(j == 0)
    def _():
        w = weights(False)

        def first(r):
            xn_ref[r, :] = _rms(x_ref[r, :], gpre_ref[...]).astype(BF16)
            o_ref[r, :] = swiglu(r, w, False)

        _for_row_chunks(nrows, first)

    @pl.when((j > 0) & (j < nj - 1))
    def _():
        w = weights(False)

        def middle(r):
            o_ref[r, :] += swiglu(r, w, False)

        _for_row_chunks(nrows, middle, nrows)

    @pl.when(j == nj - 1)
    def _():
        w = weights(True)

        def last(r):
            acc = o_ref[r, :] + swiglu(r, w, True)
            o_ref[r, :] = x_ref[r, :] + 0.5 * _rms(acc, gpost_ref[...])

        _for_row_chunks(nrows, last)


def _ffn(h, gpre, gpost, wg, wu, wd, layer):
    m = h.shape[0]
    tm = _row_tile(m, TOKEN_TILE)
    nj = pl.cdiv(D_FF, FF_TILE)
    assert nj >= 2
    return pl.pallas_call(
        functools.partial(_ffn_kernel, nj=nj),
        grid=(m // tm, nj),
        in_specs=[
            pl.BlockSpec((tm, D_MODEL), lambda i, j: (i, 0)),
            pl.BlockSpec((1, D_MODEL), lambda i, j: (0, 0)),
            pl.BlockSpec((1, D_MODEL), lambda i, j: (0, 0)),
            pl.BlockSpec((None, D_MODEL, FF_TILE), lambda i, j: (layer, 0, j)),
            pl.BlockSpec((None, D_MODEL, FF_TILE), lambda i, j: (layer, 0, j)),
            pl.BlockSpec((None, FF_TILE, D_MODEL), lambda i, j: (layer, j, 0)),
        ],
        out_specs=pl.BlockSpec((tm, D_MODEL), lambda i, j: (i, 0)),
        out_shape=jax.ShapeDtypeStruct((m, D_MODEL), F32),
        scratch_shapes=[pltpu.VMEM((tm, D_MODEL), BF16)],
        compiler_params=_params("parallel", "arbitrary"),
        name="ffn",
    )(h, gpre, gpost, wg, wu, wd)


def _mixin_kernel(x_ref, gpre_ref, w_ref, o_ref, xn_ref):
    j = pl.program_id(1)

    @pl.when(j == 0)
    def _():
        w = w_ref[...].astype(BF16)

        def first(r):
            xn_ref[r, :] = _rms(x_ref[r, :], gpre_ref[...]).astype(BF16)
            o_ref[r, :] = _dot(xn_ref[r, :], w)

        _for_row_chunks(x_ref.shape[0], first)

    @pl.when(j > 0)
    def _():
        o_ref[...] = _dot(xn_ref[...], w_ref[...].astype(BF16))


def _mixin(h, gpre, w_in, layer):
    m = h.shape[0]
    tm = MIXIN_ROWS if m % MIXIN_ROWS == 0 else _row_tile(m, TOKEN_TILE)
    return pl.pallas_call(
        _mixin_kernel,
        grid=(m // tm, IN_WIDTH // IN_TILE),
        in_specs=[
            pl.BlockSpec((tm, D_MODEL), lambda i, j: (i, 0)),
            pl.BlockSpec((1, D_MODEL), lambda i, j: (0, 0)),
            pl.BlockSpec((None, D_MODEL, IN_TILE), lambda i, j: (layer, 0, j)),
        ],
        out_specs=pl.BlockSpec((tm, IN_TILE), lambda i, j: (i, j)),
        out_shape=jax.ShapeDtypeStruct((m, IN_WIDTH), F32),
        scratch_shapes=[pltpu.VMEM((tm, D_MODEL), BF16)],
        compiler_params=_params("parallel", "arbitrary"),
        name="mixin",
    )(h, gpre, w_in)


def _s5_disc_kernel(lre_ref, lim_ref, ldt_ref, bre_ref, bim_ref,
                    are_ref, aim_ref, bbre_ref, bbim_ref):
    lam_re = lre_ref[...]
    lam_im = lim_ref[...]
    dt = jnp.exp(ldt_ref[...])
    mag = jnp.exp(lam_re * dt)
    ang = lam_im * dt
    abar_re = mag * jnp.cos(ang)
    abar_im = mag * jnp.sin(ang)
    p = abar_re - 1.0
    den = lam_re * lam_re + lam_im * lam_im
    z_re = (p * lam_re + abar_im * lam_im) / den
    z_im = (abar_im * lam_re - p * lam_im) / den
    are_ref[...] = abar_re
    aim_ref[...] = abar_im
    b_re = bre_ref[...]
    b_im = bim_ref[...]
    bbre_ref[...] = z_re * b_re - z_im * b_im
    bbim_ref[...] = z_re * b_im + z_im * b_re


def _s5_discretise(lam_re, lam_im, log_dt, b_re, b_im):
    depth = lam_re.shape[0]
    gn = S5_GROUPS * S5_STATE
    flat = lambda a: a.reshape(depth, 1, gn)
    ldt = jnp.broadcast_to(log_dt[:, :, None], lam_re.shape)
    chan_major = lambda b: jnp.transpose(b, (0, 3, 1, 2)).reshape(depth, S5_GROUP, gn)
    row = pl.BlockSpec((None, 1, gn), lambda l: (l, 0, 0))
    mat = pl.BlockSpec((None, S5_GROUP, gn), lambda l: (l, 0, 0))
    return pl.pallas_call(
        _s5_disc_kernel,
        grid=(depth,),
        in_specs=[row, row, row, mat, mat],
        out_specs=[row, row, mat, mat],
        out_shape=[jax.ShapeDtypeStruct((depth, 1, gn), F32)] * 2
        + [jax.ShapeDtypeStruct((depth, S5_GROUP, gn), F32)] * 2,
        compiler_params=_params("parallel"),
        name="s5_discretise",
    )(flat(lam_re), flat(lam_im), flat(ldt), chan_major(b_re), chan_major(b_im))


def _s5_block_weights(bb_re, bb_im, c_re, c_im):
    depth = bb_re.shape[0]
    eye = jnp.eye(S5_GB, dtype=F32)

    def in_map(bb):
        bb = bb.reshape(depth, S5_GROUP, S5_NBLK, S5_GB, S5_STATE)
        w = jnp.einsum("lcbgn,gh->lbgchn", bb, eye)
        return w.reshape(depth, S5_NBLK, S5_CW, S5_SW)

    def out_map(c):
        c = c.reshape(depth, S5_NBLK, S5_GB, S5_GROUP, S5_STATE)
        w = jnp.einsum("lbgcn,gh->lbgnhc", c, eye)
        return w.reshape(depth, S5_NBLK, S5_SW, S5_CW)

    bw = jnp.concatenate([in_map(bb_re), in_map(bb_im)], axis=-1).astype(BF16)
    cw = jnp.concatenate([out_map(c_re), -out_map(c_im)], axis=-2).astype(BF16)
    return bw, cw


def _s5_kernel(*refs, nseg, seglen, chained, ntc, nblk):
    if chained:
        (u_ref, bw_ref, cw_ref, d_ref, are_ref, aim_ref,
         y_ref, sre_ref, sim_ref, us_ref, up_ref, x_ref, yp_ref, carry_ref) = refs
    else:
        (u_ref, bw_ref, cw_ref, d_ref, are_ref, aim_ref, h0re_ref, h0im_ref, y_all_ref,
         y_ref, sre_ref, sim_ref, us_ref, up_ref, x_ref, yp_ref) = refs
    sw, cwid = S5_SW, S5_CW
    blocks = range(nblk)

    pitch = us_ref.shape[1] // nseg
    for k in blocks:
        for j in range(nseg if pitch != seglen else 1):
            n = seglen if pitch != seglen else nseg * seglen
            us_ref[k, j * pitch:j * pitch + n, :] = u_ref[j * seglen:j * seglen + n, k * cwid:(k + 1) * cwid]

    def project_in(k, after=None):
        for i in range(seglen):
            rows = us_ref.at[k][pl.ds(i, nseg, stride=pitch), :]
            up_ref[k, i * nseg:(i + 1) * nseg, :] = rows if after is None else rows + after
        x_ref[k] = _dot(up_ref[k].astype(BF16), bw_ref[k])

    def project_out(k):
        y = _dot(x_ref[k].astype(BF16), cw_ref[k]) + d_ref[k] * up_ref[k]
        yp_ref[k] = jax.nn.gelu(y)
        for i in range(seglen):
            us_ref.at[k][pl.ds(i, nseg, stride=pitch), :] = yp_ref[k, i * nseg:(i + 1) * nseg, :]

    def make_step(k, store):
        arb = jnp.broadcast_to(are_ref[k], (SUBLANES, sw))
        aib = jnp.broadcast_to(aim_ref[k], (SUBLANES, sw))

        def step(xr, xi, r0):
            br = x_ref[k, pl.ds(r0, SUBLANES), :sw]
            bi = x_ref[k, pl.ds(r0, SUBLANES), sw:]
            xr, xi = arb * xr - aib * xi + br, arb * xi + aib * xr + bi
            if store:
                x_ref[k, pl.ds(r0, SUBLANES), :sw] = xr
                x_ref[k, pl.ds(r0, SUBLANES), sw:] = xi
            return xr, xi

        return step

    if chained:
        tc = pl.program_id(2)

        @pl.when(tc == 0)
        def _():
            carry_ref[...] = jnp.zeros_like(carry_ref)

        def scan(k, after=None):
            step = make_step(k, False)
            er = ei = jnp.zeros((SUBLANES, sw), F32)
            if after is not None:
                er = er + jnp.tile(after, (1, sw // LANES))
            for i in range(seglen):
                er, ei = step(er, ei, i * SUBLANES)
            pr, pi = are_ref[k], aim_ref[k]
            for _ in range(seglen.bit_length() - 1):
                pr, pi = pr * pr - pi * pi, 2.0 * pr * pi
            cr = carry_ref[k, 0:1, :sw]
            ci = carry_ref[k, 0:1, sw:]
            starts_r, starts_i = [], []
            for j in range(SUBLANES):
                starts_r.append(cr)
                starts_i.append(ci)
                cr, ci = (pr * cr - pi * ci + er[j:j + 1, :],
                          pr * ci + pi * cr + ei[j:j + 1, :])
            carry_ref[k, 0:1, :sw] = cr
            carry_ref[k, 0:1, sw:] = ci
            step = make_step(k, True)
            xr = jnp.concatenate(starts_r, axis=0)
            xi = jnp.concatenate(starts_i, axis=0)
            for i in range(seglen):
                xr, xi = step(xr, xi, i * SUBLANES)
            bits = pltpu.bitcast(xr[:, :LANES], jnp.uint32)
            return pltpu.bitcast(lax.shift_right_logical(bits, jnp.uint32(32)), F32)

        done = {}
        project_in(0)
        project_in(1)
        done[0] = scan(0)
        for t in range(1, nblk - 1):
            project_in(t + 1, done[t - 1])
            done[t] = scan(t, done[t - 1])
            project_out(t - 1)
        scan(nblk - 1, done[nblk - 2])
        project_out(nblk - 2)
        project_out(nblk - 1)

        @pl.when(tc == ntc - 1)
        def _():
            for k in blocks:
                sre_ref[:, k * sw:(k + 1) * sw] = carry_ref[k, 0:1, :sw]
                sim_ref[:, k * sw:(k + 1) * sw] = carry_ref[k, 0:1, sw:]
    else:
        for k in blocks:
            project_in(k)

        def scan_group(sg, _):
            g0 = pl.multiple_of(sg * SUBLANES, SUBLANES)
            for k in blocks:
                step = make_step(k, True)
                xr = h0re_ref[pl.ds(g0, SUBLANES), k * sw:(k + 1) * sw]
                xi = h0im_ref[pl.ds(g0, SUBLANES), k * sw:(k + 1) * sw]
                for i in range(seglen):
                    xr, xi = step(xr, xi, pl.multiple_of(i * nseg + g0, SUBLANES))
                sre_ref[pl.ds(g0, SUBLANES), k * sw:(k + 1) * sw] = xr
                sim_ref[pl.ds(g0, SUBLANES), k * sw:(k + 1) * sw] = xi
            return 0

        lax.fori_loop(0, nseg // SUBLANES, scan_group, 0)
        for k in blocks:
            project_out(k)

    for k in blocks:
        for j in range(nseg if pitch != seglen else 1):
            n = seglen if pitch != seglen else nseg * seglen
            y_ref[j * seglen:j * seglen + n, k * cwid:(k + 1) * cwid] = us_ref[k, j * pitch:j * pitch + n, :]


def _s5_prompt(z, bw, cw, d, are, aim, batch, seq):
    nseg, seglen = SUBLANES, 64
    rows = nseg * seglen
    ntc = seq // rows
    gn = S5_GROUPS * S5_STATE
    nsb = S5_NBLK
    wspec = lambda shape: pl.BlockSpec((nsb,) + shape, lambda b, g, t: (g, 0, 0))
    vec = lambda w: wspec((1, w))
    st = pl.BlockSpec((None, 1, nsb * S5_SW), lambda b, g, t: (b, 0, g))
    slab = pltpu.VMEM((nsb, rows, S5_CW), F32)
    staging = pltpu.VMEM((nsb, nseg * (seglen + 4), S5_CW), F32)
    y, sre, sim = pl.pallas_call(
        functools.partial(_s5_kernel, nseg=nseg, seglen=seglen, chained=True, ntc=ntc, nblk=nsb),
        grid=(batch, S5_NBLK // nsb, ntc),
        in_specs=[
            pl.BlockSpec((rows, nsb * S5_CW), lambda b, g, t: (b * ntc + t, g)),
            wspec((S5_CW, 2 * S5_SW)),
            wspec((2 * S5_SW, S5_CW)),
            vec(S5_CW), vec(S5_SW), vec(S5_SW),
        ],
        out_specs=[pl.BlockSpec((rows, nsb * S5_CW), lambda b, g, t: (b * ntc + t, g)), st, st],
        out_shape=[jax.ShapeDtypeStruct((z.shape[0], S5_WIDTH), F32),
                   jax.ShapeDtypeStruct((batch, 1, gn), F32),
                   jax.ShapeDtypeStruct((batch, 1, gn), F32)],
        scratch_shapes=[staging, slab,
                        pltpu.VMEM((nsb, rows, 2 * S5_SW), F32),
                        slab,
                        pltpu.VMEM((nsb, SUBLANES, 2 * S5_SW), F32)],
        compiler_params=_params("parallel", "parallel", "arbitrary"),
        name="s5_prompt",
    )(z, bw, cw, d.reshape(S5_NBLK, 1, S5_CW), are.reshape(S5_NBLK, 1, S5_SW),
      aim.reshape(S5_NBLK, 1, S5_SW))
    shape = (batch, S5_GROUPS, S5_STATE)
    return y, sre.reshape(shape), sim.reshape(shape)


def _s5_sample(z, y_all, bw, cw, d, are, aim, h0_re, h0_im, row0, nb, nt):
    rows = nb * nt
    gn = S5_GROUPS * S5_STATE
    rb = row0 // rows
    nsb = S5_STEP_BLOCKS
    wspec = lambda shape: pl.BlockSpec((nsb,) + shape, lambda g: (g, 0, 0))
    vec = lambda w: wspec((1, w))
    st = pl.BlockSpec((nb, nsb * S5_SW), lambda g: (0, g))
    slab = pltpu.VMEM((nsb, rows, S5_CW), F32)
    y, sre, sim = pl.pallas_call(
        functools.partial(_s5_kernel, nseg=nb, seglen=nt, chained=False, ntc=1, nblk=nsb),
        grid=(S5_NBLK // nsb,),
        in_specs=[
            pl.BlockSpec((rows, nsb * S5_CW), lambda g: (rb, g)),
            wspec((S5_CW, 2 * S5_SW)),
            wspec((2 * S5_SW, S5_CW)),
            vec(S5_CW), vec(S5_SW), vec(S5_SW), st, st,
            pl.BlockSpec(memory_space=pl.ANY),
        ],
        out_specs=[pl.BlockSpec((rows, nsb * S5_CW), lambda g: (rb, g)), st, st],
        out_shape=[jax.ShapeDtypeStruct(y_all.shape, F32),
                   jax.ShapeDtypeStruct((nb, gn), F32),
                   jax.ShapeDtypeStruct((nb, gn), F32)],
        input_output_aliases={8: 0},
        scratch_shapes=[slab, slab,
                        pltpu.VMEM((nsb, rows, 2 * S5_SW), F32),
                        slab],
        compiler_params=_params("parallel"),
        name="s5_sample",
    )(z, bw, cw, d.reshape(S5_NBLK, 1, S5_CW), are.reshape(S5_NBLK, 1, S5_SW),
      aim.reshape(S5_NBLK, 1, S5_SW), h0_re.reshape(nb, gn), h0_im.reshape(nb, gn), y_all)
    shape = (nb, S5_GROUPS, S5_STATE)
    return y, sre.reshape(shape), sim.reshape(shape)


def _cumsum_rows(x, seq_len):
    ngroup = GLA_ROWS // SUBLANES
    y = x.reshape(ngroup, SUBLANES, LANES)
    r = lax.broadcasted_iota(jnp.int32, y.shape, 1)
    for d in (1, 2, 4):
        y = y + jnp.where(r >= d, pltpu.roll(y, d, axis=1), 0.0)
    if seq_len == SUBLANES:
        return y.reshape(GLA_ROWS, LANES)
    parts, acc = [], None
    for g in range(ngroup):
        if g % (seq_len // SUBLANES) == 0:
            acc = None
        part = y[g] if acc is None else y[g] + acc
        parts.append(part)
        acc = part[SUBLANES - 1:SUBLANES, :]
    return jnp.concatenate(parts, axis=0)


def _level_ref(b, level):
    half = 1 << level
    span = 2 * half
    if span >= SUBLANES:
        b3 = b.reshape(GLA_ROWS // span, span, LANES)
        return jnp.broadcast_to(b3[:, half - 1:half, :], b3.shape).reshape(GLA_ROWS, LANES)
    b3 = b.reshape(GLA_ROWS // SUBLANES, SUBLANES, LANES)
    r = lax.broadcasted_iota(jnp.int32, b3.shape, 1)
    nspan = SUBLANES // span
    ref = jnp.broadcast_to(b3[:, (nspan - 1) * span + half - 1:(nspan - 1) * span + half, :], b3.shape)
    for p in range(nspan - 2, -1, -1):
        piece = jnp.broadcast_to(b3[:, p * span + half - 1:p * span + half, :], b3.shape)
        ref = jnp.where(r < (p + 1) * span, piece, ref)
    return ref.reshape(GLA_ROWS, LANES)


def _gla_tables(seq_len):
    t = lax.broadcasted_iota(jnp.int32, (GLA_ROWS, GLA_ROWS), 0)
    s = lax.broadcasted_iota(jnp.int32, (GLA_ROWS, GLA_ROWS), 1)
    x = t ^ s
    level = jnp.zeros((GLA_ROWS, GLA_ROWS), jnp.int32)
    for k in range(1, 6):
        level = level + jnp.where(x >= (1 << k), 1, 0)
    level = jnp.where(s < t, level, jnp.where(s == t, GLA_DIAG, -1))
    rows = lax.broadcasted_iota(jnp.int32, (GLA_ROWS, LANES), 0)
    return level, rows


def _gla_block(q, fz, v, lb, states, tables, seq_len, transposed_state):
    level, rows = tables
    nseq = GLA_ROWS // seq_len
    nlev = seq_len.bit_length() - 1
    qc = jax.nn.silu(q)
    fg = lb + (1.0 - lb) * jax.nn.sigmoid(fz)
    kc = 1.0 - fg
    vb = v.astype(BF16)
    b = _cumsum_rows(jnp.log2(fg), seq_len)

    nt = (((1,), (1,)), ((), ()))
    scores = [lax.dot_general(qc.astype(BF16), kc.astype(BF16), nt, preferred_element_type=F32)]
    for lev in range(nlev):
        half = 1 << lev
        if lev == 0:
            upper = (rows & 1) == 1
            qh = jnp.where(upper, qc * fg, 0.0).astype(BF16)
            kh = jnp.where(upper, 0.0, kc).astype(BF16)
        elif half >= SUBLANES:
            shape3 = (GLA_ROWS // (2 * half), 2 * half, LANES)
            b3, q3, k3 = b.reshape(shape3), qc.reshape(shape3), kc.reshape(shape3)
            ref = b3[:, half - 1:half, :]
            zero = jnp.zeros((shape3[0], half, LANES), F32)
            qh = jnp.concatenate([zero, q3[:, half:, :] * jnp.exp2(b3[:, half:, :] - ref)], axis=1)
            kh = jnp.concatenate([k3[:, :half, :] * jnp.exp2(ref - b3[:, :half, :]), zero], axis=1)
            qh = qh.reshape(GLA_ROWS, LANES).astype(BF16)
            kh = kh.reshape(GLA_ROWS, LANES).astype(BF16)
        else:
            w = jnp.exp2(-jnp.abs(b - _level_ref(b, lev)))
            upper = ((rows >> lev) & 1) == 1
            qh = jnp.where(upper, qc * w, 0.0).astype(BF16)
            kh = jnp.where(upper, 0.0, kc * w).astype(BF16)
        scores.append(lax.dot_general(qh, kh, nt, preferred_element_type=F32))
    att_rows = []
    for g in range(GLA_ROWS // SUBLANES):
        rs = slice(g * SUBLANES, (g + 1) * SUBLANES)
        lv = level[rs]
        a = jnp.where(lv == GLA_DIAG, scores[0][rs], 0.0)
        for lev in range(nlev):
            if lev < 3 or (g >> (lev - 3)) & 1:
                a = jnp.where(lv == lev, scores[lev + 1][rs], a)
        att_rows.append(a)
    o = _dot(jnp.concatenate(att_rows, axis=0).astype(BF16), vb)

    b3 = b.reshape(nseq, seq_len, LANES)
    blast = jnp.broadcast_to(b3[:, seq_len - 1:seq_len, :], b3.shape).reshape(GLA_ROWS, LANES)
    qe = (qc * jnp.exp2(b)).astype(BF16)
    kd = (kc * jnp.exp2(blast - b)).astype(BF16)
    tn = (((0,), (0,)), ((), ()))
    o_inter, new_states = [], []
    for n in range(nseq):
        sl = slice(n * seq_len, (n + 1) * seq_len)
        s_n = states[n]
        e_row = jnp.exp2(blast[n * seq_len:n * seq_len + 1, :])
        if transposed_state:
            o_inter.append(lax.dot_general(qe[sl], s_n.astype(BF16), nt, preferred_element_type=F32))
            upd = lax.dot_general(vb[sl], kd[sl], tn, preferred_element_type=F32)
            new_states.append(e_row * s_n + upd)
        else:
            o_inter.append(_dot(qe[sl], s_n.astype(BF16)))
            upd = lax.dot_general(kd[sl], vb[sl], tn, preferred_element_type=F32)
            e_col = jnp.transpose(jnp.broadcast_to(e_row, (HGRN_DK, LANES)))
            new_states.append(e_col * s_n + upd)
    o = o + (o_inter[0] if nseq == 1 else jnp.concatenate(o_inter, axis=0))
    return o, new_states


def _lower_bound(lbp, layer):
    if layer == 0:
        return jnp.zeros((1, lbp.shape[1]), F32)
    e = jnp.exp(lbp - jnp.max(lbp, axis=0, keepdims=True))
    p = e / jnp.sum(e, axis=0, keepdims=True)
    return jnp.sum(p[1:layer + 1, :], axis=0, keepdims=True)


def _gla_finish(o, g, gain):
    o = o * lax.rsqrt(jnp.mean(o * o, axis=-1, keepdims=True) + EPS)
    return o * gain * jax.nn.silu(g)


def _hgrn_prompt_kernel(q_ref, f_ref, v_ref, g_ref, lbp_ref, gain_ref, o_ref, sout_ref, s_ref,
                        *, layer, nchunk, ntb):
    tb = pl.program_id(2)

    @pl.when(tb == 0)
    def _():
        s_ref[...] = jnp.zeros_like(s_ref)

    tables = _gla_tables(GLA_ROWS)
    lb = _lower_bound(lbp_ref[...], layer)
    gain = gain_ref[layer:layer + 1, :]

    def chunk(c, _):
        r = pl.ds(pl.multiple_of(c * GLA_ROWS, GLA_ROWS), GLA_ROWS)
        for hh in range(HGRN_HB):
            cs = slice(hh * LANES, (hh + 1) * LANES)
            o, (s_new,) = _gla_block(q_ref[r, cs], f_ref[r, cs], v_ref[r, cs], lb[:, cs],
                                     [s_ref[hh]], tables, GLA_ROWS, True)
            s_ref[hh] = s_new
            o_ref[r, cs] = _gla_finish(o, g_ref[r, cs], gain[:, cs])
        return 0

    lax.fori_loop(0, nchunk, chunk, 0, unroll=4)

    @pl.when(tb == ntb - 1)
    def _():
        for hh in range(HGRN_HB):
            sout_ref[hh] = jnp.transpose(s_ref[hh])


def _hgrn_prompt(z, lbp, gain, layer, batch, seq):
    tb_rows = 1024 if seq % 1024 == 0 else 512
    ntb = seq // tb_rows
    wb = HGRN_HB * LANES
    nhb = HGRN_HEADS // HGRN_HB

    def zspec(k):
        c0 = (S5_WIDTH + k * HGRN_WIDTH) // wb
        return pl.BlockSpec((tb_rows, wb), lambda b, h, t: (b * ntb + t, c0 + h))

    par = pl.BlockSpec((lbp.shape[0], wb), lambda b, h, t: (0, h))
    return pl.pallas_call(
        functools.partial(_hgrn_prompt_kernel, layer=layer, nchunk=tb_rows // GLA_ROWS, ntb=ntb),
        grid=(batch, nhb, ntb),
        in_specs=[zspec(0), zspec(1), zspec(2), zspec(3), par, par],
        out_specs=[pl.BlockSpec((tb_rows, wb), lambda b, h, t: (b * ntb + t, h)),
                   pl.BlockSpec((None, HGRN_HB, HGRN_DK, HGRN_DV), lambda b, h, t: (b, h, 0, 0))],
        out_shape=[jax.ShapeDtypeStruct((z.shape[0], HGRN_WIDTH), F32),
                   jax.ShapeDtypeStruct((batch, HGRN_HEADS, HGRN_DK, HGRN_DV), F32)],
        scratch_shapes=[pltpu.VMEM((HGRN_HB, HGRN_DK, HGRN_DV), F32)],
        compiler_params=_params("parallel", "parallel", "arbitrary"),
        name="hgrn_prompt",
    )(z, z, z, z, lbp, gain)


def _hgrn_sample_kernel(q_ref, f_ref, v_ref, g_ref, lbp_ref, gain_ref, s0_ref, *rest, layer, nt):
    o_ref, sout_ref = rest[-2:]
    nseq = GLA_ROWS // nt
    tables = _gla_tables(nt)
    lb = _lower_bound(lbp_ref[...], layer)
    gain = gain_ref[layer:layer + 1, :]
    for hh in range(HGRN_HB):
        cs = slice(hh * LANES, (hh + 1) * LANES)
        o, new_states = _gla_block(q_ref[:, cs], f_ref[:, cs], v_ref[:, cs], lb[:, cs],
                                   [s0_ref[n, hh] for n in range(nseq)], tables, nt, False)
        for n in range(nseq):
            sout_ref[n, hh] = new_states[n].astype(sout_ref.dtype)
        o_ref[:, cs] = _gla_finish(o, g_ref[:, cs], gain[:, cs])


def _hgrn_sample(z, y_all, lbp, gain, s0, s_all, layer, row0, nb, nt):
    nseq = GLA_ROWS // nt
    rb0 = row0 // GLA_ROWS
    wb = HGRN_HB * LANES

    def zspec(k):
        c0 = (S5_WIDTH + k * HGRN_WIDTH) // wb
        return pl.BlockSpec((GLA_ROWS, wb), lambda h, i: (rb0 + i, c0 + h))

    par = pl.BlockSpec((lbp.shape[0], wb), lambda h, i: (0, h))
    sspec = pl.BlockSpec((None, nseq, HGRN_HB, HGRN_DK, HGRN_DV), lambda h, i: (layer, i, h, 0, 0))
    anyspec = pl.BlockSpec(memory_space=pl.ANY)
    carried = (y_all,) if s_all is None else (y_all, s_all)
    aliases = {7: 0} if s_all is None else {7: 0, 8: 1}
    return pl.pallas_call(
        functools.partial(_hgrn_sample_kernel, layer=layer, nt=nt),
        grid=(HGRN_HEADS // HGRN_HB, nb // nseq),
        in_specs=[zspec(0), zspec(1), zspec(2), zspec(3), par, par, sspec] + [anyspec] * len(carried),
        out_specs=[pl.BlockSpec((GLA_ROWS, wb), lambda h, i: (rb0 + i, h)), sspec],
        out_shape=[jax.ShapeDtypeStruct(y_all.shape, F32),
                   jax.ShapeDtypeStruct(s0.shape, s0.dtype)],
        input_output_aliases=aliases,
        compiler_params=_params("parallel", "parallel"),
        name="hgrn_sample",
    )(z, z, z, z, lbp, gain, s0, *carried)


def _mixout_kernel(h_ref, ys_ref, yh_ref, wglu_ref, bglu_ref, wo_ref, gpost_ref, o_ref):
    def rows(r):
        ys = ys_ref[r, :]
        gate = jax.nn.sigmoid(_dot(ys.astype(BF16), wglu_ref[...]) + bglu_ref[...])
        out = (_dot((ys * gate).astype(BF16), wo_ref[:S5_WIDTH, :])
               + _dot(yh_ref[r, :].astype(BF16), wo_ref[S5_WIDTH:, :]))
        o_ref[r, :] = h_ref[r, :] + _rms(out, gpost_ref[...])

    _for_row_chunks(h_ref.shape[0], rows)


def _mixout(h, ys, yh, w_glu, b_glu, w_out, gpost):
    m = h.shape[0]
    tm = _row_tile(m)
    full = lambda shape: pl.BlockSpec(shape, lambda i: (0, 0))
    return pl.pallas_call(
        _mixout_kernel,
        grid=(m // tm,),
        in_specs=[
            pl.BlockSpec((tm, D_MODEL), lambda i: (i, 0)),
            pl.BlockSpec((tm, S5_WIDTH), lambda i: (i, 0)),
            pl.BlockSpec((tm, HGRN_WIDTH), lambda i: (i, 0)),
            full((S5_WIDTH, S5_WIDTH)), full((1, S5_WIDTH)),
            full((D_MODEL, D_MODEL)), full((1, D_MODEL)),
        ],
        out_specs=pl.BlockSpec((tm, D_MODEL), lambda i: (i, 0)),
        out_shape=jax.ShapeDtypeStruct((m, D_MODEL), F32),
        compiler_params=_params("parallel"),
        name="mixout",
    )(h, ys, yh, w_glu, b_glu, w_out, gpost)


def kernel(x_prompt, x_sample, state_s5_re, state_s5_im, state_hgrn, norm_pre, norm_post, ffn1_w_gate, ffn1_w_up, ffn1_w_down, ffn2_w_gate, ffn2_w_up, ffn2_w_down, w_in, w_out, s5_lam_re, s5_lam_im, s5_log_dt, s5_b_re, s5_b_im, s5_c_re, s5_c_im, s5_d, s5_w_glu, s5_b_glu, hgrn_lb, hgrn_norm):
    batch, seq, _ = x_prompt.shape
    nb, nt, _ = x_sample.shape
    depth = norm_pre.shape[0]
    mp = batch * seq
    sdt = state_hgrn.dtype

    h = jnp.concatenate([x_prompt.reshape(mp, D_MODEL), x_sample.reshape(nb * nt, D_MODEL)], axis=0)

    are, aim, bb_re, bb_im = _s5_discretise(s5_lam_re, s5_lam_im, s5_log_dt, s5_b_re, s5_b_im)
    bw, cw = _s5_block_weights(bb_re, bb_im, s5_c_re, s5_c_im)

    vec = lambda a: a.reshape(1, -1)

    small = ("re_p", "im_p", "h_p", "re_s", "im_s")
    outs = {k: [] for k in small}
    h_s_all = None
    for l in range(depth):
        h = _ffn(h, vec(norm_pre[l, 0]), vec(norm_post[l, 0]),
                 ffn1_w_gate, ffn1_w_up, ffn1_w_down, l)

        z = _mixin(h, vec(norm_pre[l, 1]), w_in, l)
        d = vec(s5_d[l])
        ys, re_p, im_p = _s5_prompt(z, bw[l], cw[l], d, are[l], aim[l], batch, seq)
        ys, re_s, im_s = _s5_sample(z, ys, bw[l], cw[l], d, are[l], aim[l],
                                    state_s5_re[l], state_s5_im[l], mp, nb, nt)
        yh, h_p = _hgrn_prompt(z, hgrn_lb, hgrn_norm, l, batch, seq)
        yh, h_s_all = _hgrn_sample(z, yh, hgrn_lb, hgrn_norm, state_hgrn, h_s_all, l, mp, nb, nt)
        h = _mixout(h, ys, yh, s5_w_glu[l].astype(BF16), vec(s5_b_glu[l]), w_out[l].astype(BF16),
                    vec(norm_post[l, 1]))

        h = _ffn(h, vec(norm_pre[l, 2]), vec(norm_post[l, 2]),
                 ffn2_w_gate, ffn2_w_up, ffn2_w_down, l)
        for k, a in zip(small, (re_p, im_p, h_p, re_s, im_s)):
            outs[k].append(a)

    stack = lambda k: jnp.stack(outs[k]).astype(sdt)
    return (h[:mp].reshape(batch, seq, D_MODEL), h[mp:].reshape(nb, nt, D_MODEL),
            stack("re_p"), stack("im_p"), stack("h_p"),
            stack("re_s"), stack("im_s"), h_s_all)
```
